```python
import math
import jax
import jax.numpy as jnp
from jax import lax
import numpy as np

D_MODEL = 1024
BATCH = 4
SEQ = 8192
DEPTH = 1

GRID_W = 64
CTX_LEN = 256
NA_HEADS = 8
NA_HEAD_DIM = 64
NA_WIDTH = NA_HEADS * NA_HEAD_DIM
NA_KH = 8
NA_KW = 16
DN_HEADS = 8
DN_DK = 64
DN_DV = 64
DN_QK_W = DN_HEADS * DN_DK
DN_WIDTH = DN_HEADS * DN_DV
DN_CONV_W = 2 * DN_QK_W + DN_WIDTH
DN_CHUNK = 64
CONV_K = 3
ROPE_BASE = 10000.0
EPS = 1e-6
PROJ_WIDTH = 4 * NA_WIDTH + DN_CONV_W + DN_WIDTH + 4 * DN_HEADS + 2 * D_MODEL

kernel_name = 'hybrid_na_gdn_prefix_block'


def _rmsnorm(x, w):
    xf = x.astype(jnp.float32)
    y = xf * lax.rsqrt(jnp.mean(xf * xf, axis=-1, keepdims=True) + EPS)
    return (y * w.astype(jnp.float32)).astype(x.dtype)


def _l2norm(x):
    xf = x.astype(jnp.float32)
    return (xf * lax.rsqrt(jnp.sum(xf * xf, axis=-1, keepdims=True) + EPS)).astype(x.dtype)


def _heads(a, n_heads):
    return a.reshape(a.shape[:-1] + (n_heads, a.shape[-1] // n_heads))


def _rope_1d(x, pos):
    half = x.shape[-1] // 2
    freqs = ROPE_BASE ** (-jnp.arange(half, dtype=jnp.float32) / half)
    ang = pos.astype(jnp.float32)[:, None] * freqs[None, :]
    cos = jnp.concatenate([jnp.cos(ang), jnp.cos(ang)], -1)[None, :, None, :]
    sin = jnp.concatenate([jnp.sin(ang), jnp.sin(ang)], -1)[None, :, None, :]
    xf = x.astype(jnp.float32)
    rot = jnp.concatenate([-xf[..., half:], xf[..., :half]], -1)
    return (xf * cos + rot * sin).astype(x.dtype)


def _rope_2d(x, pos_row, pos_col):
    d = x.shape[-1] // 2
    return jnp.concatenate([_rope_1d(x[..., :d], pos_row), _rope_1d(x[..., d:], pos_col)], -1)


def _short_conv(x, w):
    return lax.conv_general_dilated(
        x, w.astype(x.dtype)[:, None, :], window_strides=(1,),
        padding=[(CONV_K // 2, CONV_K // 2)],
        dimension_numbers=('NWC', 'WIO', 'NWC'),
        feature_group_count=x.shape[-1])


def _split_proj(p):
    sizes = (NA_WIDTH, NA_WIDTH, NA_WIDTH, NA_WIDTH, DN_CONV_W, DN_WIDTH,
             2 * DN_HEADS, 2 * DN_HEADS, D_MODEL, D_MODEL)
    return jnp.split(p, np.cumsum(sizes)[:-1].tolist(), axis=-1)


def _na_latent(q, k, v, kc, vc, rpb):
    B, T, H, Dh = q.shape
    rows = T // GRID_W
    kh = min(NA_KH, rows)
    kg = k.reshape(B, rows, GRID_W, H, Dh)
    vg = v.reshape(B, rows, GRID_W, H, Dh)
    qg = q.reshape(B, rows, GRID_W, H, Dh)
    cq = jnp.arange(GRID_W)
    c0 = jnp.clip(cq - NA_KW // 2, 0, GRID_W - NA_KW)
    col_in = (cq[None, :] >= c0[:, None]) & (cq[None, :] < c0[:, None] + NA_KW)
    dx = jnp.clip(cq[None, :] - cq[:, None], -(NA_KW - 1), NA_KW - 1) + (NA_KW - 1)
    rpb_cols = rpb[:, :, dx]

    def row_step(args):
        r, q_r = args
        r0 = jnp.clip(r - kh // 2, 0, rows - kh)
        k_band = lax.dynamic_slice_in_dim(kg, r0, kh, axis=1)
        v_band = lax.dynamic_slice_in_dim(vg, r0, kh, axis=1)
        dy = r0 + jnp.arange(kh) - r + (NA_KH - 1)
        bias = jnp.take(rpb_cols, dy, axis=1).transpose(0, 2, 1, 3)
        s_win = jnp.einsum('bqhd,bywhd->bhqyw', q_r, k_band).astype(jnp.float32) + bias[None].astype(jnp.float32)
        s_win = jnp.where(col_in[:, None, :], s_win, -jnp.inf)
        s_ctx = jnp.einsum('bqhd,bchd->bhqc', q_r, kc).astype(jnp.float32)
        s = jnp.concatenate([s_win.reshape(B, H, GRID_W, kh * GRID_W), s_ctx], axis=-1)
        pr = jax.nn.softmax(s, axis=-1).astype(v.dtype)
        p_win = pr[..., :kh * GRID_W].reshape(B, H, GRID_W, kh, GRID_W)
        p_ctx = pr[..., kh * GRID_W:]
        return (jnp.einsum('bhqyw,bywhd->bqhd', p_win, v_band)
                + jnp.einsum('bhqc,bchd->bqhd', p_ctx, vc))

    o = lax.map(row_step, (jnp.arange(rows), jnp.moveaxis(qg, 1, 0)))
    return jnp.moveaxis(o, 0, 1).reshape(B, T, H, Dh)


def _dense_attention(q, k, v):
    s = jnp.einsum('bqhd,bkhd->bhqk', q, k).astype(jnp.float32)
    p = jax.nn.softmax(s, axis=-1).astype(v.dtype)
    return jnp.einsum('bhqk,bkhd->bqhd', p, v)


def _gdn_chunked(q, k, v, g, beta, s0):
    out_dtype = v.dtype
    f32 = jnp.float32
    B, T, H, DK = q.shape
    DV = v.shape[-1]
    C = DN_CHUNK
    N = T // C

    def blk(a):
        a = a.astype(f32).reshape((B, N, C, H) + a.shape[3:])
        return jnp.swapaxes(a, 2, 3)

    q, k, v, g, beta = blk(q), blk(k), blk(v), blk(g), blk(beta)
    gc = jnp.cumsum(g, axis=-1)
    incl = jnp.tril(jnp.ones((C, C), bool))
    strict = jnp.tril(jnp.ones((C, C), bool), -1)
    decay = jnp.exp(jnp.where(incl, gc[..., :, None] - gc[..., None, :], -jnp.inf))
    kb = k * beta[..., None]
    lower = jnp.where(strict, jnp.einsum('bnhid,bnhjd->bnhij', kb, k) * decay, 0.0)
    a_mat = lower + jnp.eye(C, dtype=f32)
    rhs = jnp.concatenate([v * beta[..., None], kb * jnp.exp(gc)[..., None]], axis=-1)
    sol = lax.linalg.triangular_solve(a_mat, rhs, left_side=True, lower=True, unit_diagonal=True)
    u, wk = sol[..., :DV], sol[..., DV:]
    qk = jnp.einsum('bnhid,bnhjd->bnhij', q, k) * decay
    q_dec = q * jnp.exp(gc)[..., None]
    k_dec = k * jnp.exp(gc[..., -1:] - gc)[..., None]
    g_tot = jnp.exp(gc[..., -1])

    def step(S, xs):
        qk_i, qd_i, kd_i, u_i, w_i, gt_i = xs
        v_new = u_i - jnp.einsum('bhck,bhkv->bhcv', w_i, S)
        o_i = jnp.einsum('bhck,bhkv->bhcv', qd_i, S) + jnp.einsum('bhij,bhjv->bhiv', qk_i, v_new)
        S = S * gt_i[..., None, None] + jnp.einsum('bhck,bhcv->bhkv', kd_i, v_new)
        return S, o_i

    xs = (jnp.moveaxis(qk, 1, 0), jnp.moveaxis(q_dec, 1, 0), jnp.moveaxis(k_dec, 1, 0),
          jnp.moveaxis(u, 1, 0), jnp.moveaxis(wk, 1, 0), jnp.moveaxis(g_tot, 1, 0))
    s_fin, o = lax.scan(step, s0.astype(f32), xs)
    o = jnp.transpose(o, (1, 0, 3, 2, 4)).reshape(B, T, H, DV)
    return o.astype(out_dtype), s_fin


def _dn_prepare(qkv_raw, b_raw, a_raw, w, pos):
    B, T, _ = qkv_raw.shape
    f32 = jnp.float32
    qkv = jax.nn.silu(_short_conv(qkv_raw, w['conv_w']))
    q, k, v = jnp.split(qkv, [DN_QK_W, 2 * DN_QK_W], axis=-1)
    q = _l2norm(_heads(q, DN_HEADS))
    k = _l2norm(_heads(k, DN_HEADS))
    v = _heads(v, DN_HEADS)
    if pos is not None:
        q = _rope_2d(q, pos[0], pos[1])
        k = _rope_2d(k, pos[0], pos[1])
    q = q * DN_DK ** -0.5
    beta = jax.nn.sigmoid(b_raw.astype(f32)).reshape(B, T, 2, DN_HEADS)
    g = -jnp.exp(w['dn_A_log'].astype(f32)) * jax.nn.softplus(
        a_raw.astype(f32).reshape(B, T, 2, DN_HEADS) + w['dn_dt_bias'].astype(f32))
    return q, k, v, beta, g


def _bidir_delta(q, k, v, beta, g, s0_f, s0_b):
    rev = lambda a: jnp.flip(a, axis=1)
    o_f, s_f = _gdn_chunked(q, k, v, g[:, :, 0], beta[:, :, 0], s0_f)
    o_b, s_b = _gdn_chunked(rev(q), rev(k), rev(v), rev(g[:, :, 1]), rev(beta[:, :, 1]), s0_b)
    return o_f + rev(o_b), s_f, s_b


def _merge(o_na, z_na, o_dn, z_dn, g_na, g_dn, w):
    B, T = o_na.shape[:2]
    u_na = (o_na.reshape(B, T, NA_WIDTH) * jax.nn.silu(z_na)) @ w['w_o_na']
    o_dn = _rmsnorm(o_dn, w['dn_norm_w']).reshape(B, T, DN_WIDTH)
    u_dn = (o_dn * jax.nn.silu(z_dn)) @ w['w_o_dn']
    y = jax.nn.sigmoid(g_na) * u_na + jax.nn.sigmoid(g_dn) * u_dn
    return y @ w['w_out']


def _layer(x, ctx, c, c_ctx, w, last):
    B, T, _ = x.shape
    t = jnp.arange(T)
    pos = (t // GRID_W, t % GRID_W)
    shift, scale, gate = jnp.split(jax.nn.silu(c) @ w['mod_w'] + w['mod_b'], 3, axis=-1)
    shift_c, scale_c, gate_c = jnp.split(jax.nn.silu(c_ctx) @ w['mod_w'] + w['mod_b'], 3, axis=-1)
    h = _rmsnorm(x, w['norm_w']) * (1.0 + scale[:, None]) + shift[:, None]
    hc = _rmsnorm(ctx, w['norm_w']) * (1.0 + scale_c) + shift_c
    na_q, na_k, na_v, na_z, dn_qkv, dn_z, dn_b, dn_a, g_na, g_dn = _split_proj(h @ w['w_in'])
    na_qc, na_kc, na_vc, na_zc, dn_qkvc, dn_zc, dn_bc, dn_ac, g_nac, g_dnc = _split_proj(hc @ w['w_in'])

    q = _rmsnorm(_heads(na_q, NA_HEADS), w['na_q_norm']) * NA_HEAD_DIM ** -0.5
    k = _rmsnorm(_heads(na_k, NA_HEADS), w['na_k_norm'])
    v = _heads(na_v, NA_HEADS)
    kc = _rmsnorm(_heads(na_kc, NA_HEADS), w['na_k_norm'])
    vc = _heads(na_vc, NA_HEADS)
    o_na = _na_latent(q, k, v, kc, vc, w['na_rpb'])

    dq, dk, dv, dbeta, dg = _dn_prepare(dn_qkv, dn_b, dn_a, w, pos)
    dqc, dkc, dvc, dbetac, dgc = _dn_prepare(dn_qkvc, dn_bc, dn_ac, w, None)
    zeros = jnp.zeros((ctx.shape[0], DN_HEADS, DN_DK, DN_DV), jnp.float32)
    o_dnc, s_f, s_b = _bidir_delta(dqc, dkc, dvc, dbetac, dgc, zeros, zeros)
    o_dn, _, _ = _bidir_delta(dq, dk, dv, dbeta, dg, s_f, s_b)

    x = x + gate[:, None] * _merge(o_na, na_z, o_dn, dn_z, g_na, g_dn, w)
    if not last:
        qc = _rmsnorm(_heads(na_qc, NA_HEADS), w['na_q_norm']) * NA_HEAD_DIM ** -0.5
        o_nac = _dense_attention(qc, kc, vc)
        ctx = ctx + gate_c * _merge(o_nac, na_zc, o_dnc, dn_zc, g_nac, g_dnc, w)
    return x, ctx


def setup_inputs(seed: int = 0) -> dict:
    key = jax.random.key(seed)
    ks = jax.random.split(key, 20)
    f32 = jnp.float32

    def nrm(k, shape, s):
        return jax.random.normal(k, shape, f32) * s

    x = nrm(ks[0], (BATCH, SEQ, D_MODEL), 1.0)
    c = nrm(ks[1], (BATCH, D_MODEL), 1.0)
    ctx = nrm(ks[2], (BATCH, CTX_LEN, D_MODEL), 1.0)
    c_ctx = nrm(ks[3], (D_MODEL,), 1.0)
    mod_w = nrm(ks[4], (DEPTH, D_MODEL, 3 * D_MODEL), 0.5 * D_MODEL ** -0.5)
    mod_b = nrm(ks[5], (DEPTH, 3 * D_MODEL), 0.01)
    norm_w = 1.0 + nrm(ks[6], (DEPTH, D_MODEL), 0.01)
    w_in = nrm(ks[7], (DEPTH, D_MODEL, PROJ_WIDTH), D_MODEL ** -0.5)
    conv_w = nrm(ks[8], (DEPTH, CONV_K, DN_CONV_W), CONV_K ** -0.5)
    na_q_norm = 1.0 + nrm(ks[9], (DEPTH, NA_HEAD_DIM), 0.01)
    na_k_norm = 1.0 + nrm(ks[10], (DEPTH, NA_HEAD_DIM), 0.01)
    na_rpb = nrm(ks[11], (DEPTH, NA_HEADS, 2 * NA_KH - 1, 2 * NA_KW - 1), 0.1)
    dn_A_log = jnp.log(jax.random.uniform(ks[12], (DEPTH, 2, DN_HEADS), f32, 1.0, 16.0))
    dt = jnp.exp(jax.random.uniform(ks[13], (DEPTH, 2, DN_HEADS), f32, math.log(1e-3), math.log(1e-1)))
    dn_dt_bias = dt + jnp.log(-jnp.expm1(-dt))
    dn_norm_w = 1.0 + nrm(ks[14], (DEPTH, DN_DV), 0.01)
    w_o_na = nrm(ks[15], (DEPTH, NA_WIDTH, D_MODEL), NA_WIDTH ** -0.5)
    w_o_dn = nrm(ks[16], (DEPTH, DN_WIDTH, D_MODEL), DN_WIDTH ** -0.5)
    w_out = nrm(ks[17], (DEPTH, D_MODEL, D_MODEL), D_MODEL ** -0.5)
    return {'x': x, 'c': c, 'ctx': ctx, 'c_ctx': c_ctx, 'mod_w': mod_w, 'mod_b': mod_b,
            'norm_w': norm_w, 'w_in': w_in, 'conv_w': conv_w, 'na_q_norm': na_q_norm,
            'na_k_norm': na_k_norm, 'na_rpb': na_rpb, 'dn_A_log': dn_A_log, 'dn_dt_bias': dn_dt_bias,
            'dn_norm_w': dn_norm_w, 'w_o_na': w_o_na, 'w_o_dn': w_o_dn, 'w_out': w_out}


def reference(x, c, ctx, c_ctx, mod_w, mod_b, norm_w, w_in, conv_w, na_q_norm, na_k_norm, na_rpb,
              dn_A_log, dn_dt_bias, dn_norm_w, w_o_na, w_o_dn, w_out):
    for l in range(DEPTH):
        w = {'mod_w': mod_w[l], 'mod_b': mod_b[l], 'norm_w': norm_w[l], 'w_in': w_in[l],
             'conv_w': conv_w[l], 'na_q_norm': na_q_norm[l], 'na_k_norm': na_k_norm[l],
             'na_rpb': na_rpb[l], 'dn_A_log': dn_A_log[l], 'dn_dt_bias': dn_dt_bias[l],
             'dn_norm_w': dn_norm_w[l], 'w_o_na': w_o_na[l], 'w_o_dn': w_o_dn[l], 'w_out': w_out[l]}
        x, ctx = _layer(x, ctx, c, c_ctx, w, l == DEPTH - 1)
    return x
```

```python
import functools

import jax
import jax.numpy as jnp
import numpy as np
from jax import lax
from jax.experimental import pallas as pl
from jax.experimental.pallas import tpu as pltpu

F32 = jnp.float32
BF16 = jnp.bfloat16
EPS = 1e-6
GRID_W = 64
HEADS = 8
HEAD_DIM = 64
WIDTH = HEADS * HEAD_DIM
NA_KH = 8
NA_KW = 16
CHUNK = 64
CONV_K = 3
ROPE_BASE = 10000.0
GROUP = 4
GW = GROUP * HEAD_DIM
NEG_INF = float("-inf")
HIGHEST = lax.Precision.HIGHEST
VMEM_LIMIT = 56 * 1024 * 1024


def _dot(a, b, **kw):
    return jnp.dot(a, b, preferred_element_type=F32, **kw)


def _dot_nt(a, b):
    return lax.dot_general(a, b, (((1,), (1,)), ((), ())), preferred_element_type=F32)


def _dot_tn(a, b):
    return lax.dot_general(a, b, (((0,), (0,)), ((), ())), preferred_element_type=F32)


def _seg_sum(x, e_ref):
    hi = x.astype(BF16)
    lo = (x - hi.astype(F32)).astype(BF16)
    e = e_ref[...]
    return _dot(hi, e) + _dot(lo, e)


def _silu(x):
    return x * jax.nn.sigmoid(x)


def _mod_body(c_ref, w_ref, b_ref, o_ref):
    o_ref[...] = _dot(_silu(c_ref[...]), w_ref[...], precision=HIGHEST) + b_ref[...]


def _modulation(cc, mod_w, mod_b):
    rows, d = cc.shape
    n = mod_w.shape[1]
    tn = 512
    return pl.pallas_call(
        _mod_body,
        grid=(n // tn,),
        in_specs=[pl.BlockSpec((rows, d), lambda j: (0, 0)),
                  pl.BlockSpec((d, tn), lambda j: (0, j)),
                  pl.BlockSpec((1, tn), lambda j: (0, j))],
        out_specs=pl.BlockSpec((rows, tn), lambda j: (0, j)),
        out_shape=jax.ShapeDtypeStruct((rows, n), F32),
        compiler_params=pltpu.CompilerParams(dimension_semantics=("parallel",)),
        name="mod",
    )(cc, mod_w, mod_b.reshape(1, n))


def _proj_body(x_ref, scale_ref, shift_ref, nw_ref, w_ref, wba_ref, qw_ref, kw_ref, e_ref, alog_ref, dtb_ref,
               q_ref, k_ref, v_ref, gzna_ref, dn_ref, gzdn_ref, bg_ref, sgna_ref, sgdn_ref):
    x = x_ref[0]
    xn = x * lax.rsqrt(jnp.mean(x * x, axis=-1, keepdims=True) + EPS)
    h = (xn * nw_ref[...]) * (1.0 + scale_ref[0]) + shift_ref[0]
    hb = h.astype(BF16)

    def mm(lo, hi):
        return _dot(hb, w_ref[:, lo:hi])

    def head_rms(a, w_row):
        return a * lax.rsqrt(_seg_sum(a * a, e_ref) * (1.0 / HEAD_DIM) + EPS) * w_row

    w = WIDTH
    q_ref[0] = (head_rms(mm(0, w), qw_ref[...]) * HEAD_DIM ** -0.5).astype(BF16)
    k_ref[0] = head_rms(mm(w, 2 * w), kw_ref[...]).astype(BF16)
    v_ref[0] = mm(2 * w, 3 * w).astype(BF16)
    gzna_ref[0] = _silu(mm(3 * w, 4 * w)).astype(BF16)
    dn_ref[0] = mm(4 * w, 7 * w)
    gzdn_ref[0] = _silu(mm(7 * w, 8 * w)).astype(BF16)
    sgna_ref[0] = jax.nn.sigmoid(mm(8 * w, 10 * w)).astype(BF16)
    sgdn_ref[0] = jax.nn.sigmoid(mm(10 * w, 12 * w)).astype(BF16)

    ba = _dot(hb, wba_ref[...])
    lane = lax.broadcasted_iota(jnp.int32, ba.shape, 1)
    a = ba + dtb_ref[...]
    softplus = jnp.maximum(a, 0.0) + jnp.log1p(jnp.exp(-jnp.abs(a)))
    g = -jnp.exp(alog_ref[...]) * softplus
    bg_ref[0] = jnp.where(lane < 2 * HEADS, jax.nn.sigmoid(ba), jnp.where(lane < 4 * HEADS, g, 0.0))


def _project(x, scale, shift, nw, w_main, w_ba, qw, kw, e64, alog, dtb, tm):
    b, t, d = x.shape
    w = WIDTH
    tok = lambda width: pl.BlockSpec((1, tm, width), lambda bi, i: (bi, i, 0))
    row = lambda width: pl.BlockSpec((1, width), lambda bi, i: (0, 0))
    per_batch = pl.BlockSpec((1, 1, d), lambda bi, i: (bi, 0, 0))
    full = lambda a: pl.BlockSpec(a.shape, lambda bi, i: (0, 0))
    sds = lambda width, dt: jax.ShapeDtypeStruct((b, t, width), dt)
    return pl.pallas_call(
        _proj_body,
        grid=(b, t // tm),
        in_specs=[tok(d), per_batch, per_batch, row(d), full(w_main), full(w_ba), row(w), row(w), full(e64),
                  row(128), row(128)],
        out_specs=[tok(w), tok(w), tok(w), tok(w), tok(3 * w), tok(w), tok(128), tok(2 * w), tok(2 * w)],
        out_shape=[sds(w, BF16), sds(w, BF16), sds(w, BF16), sds(w, BF16), sds(3 * w, F32), sds(w, BF16),
                   sds(128, F32), sds(2 * w, BF16), sds(2 * w, BF16)],
        compiler_params=pltpu.CompilerParams(dimension_semantics=("parallel", "parallel"),
                                             vmem_limit_bytes=VMEM_LIMIT),
        name="proj",
    )(x, scale, shift, nw, w_main, w_ba, qw, kw, e64, alog, dtb)


def _bias_body(rpb_ref, o_ref):
    h = pl.program_id(0)
    n_dy, n_dx = 2 * NA_KH - 1, 2 * NA_KW - 1
    cq = lax.broadcasted_iota(jnp.int32, (GRID_W, GRID_W), 0)
    ck = lax.broadcasted_iota(jnp.int32, (GRID_W, GRID_W), 1)
    c0 = jnp.clip(cq - NA_KW // 2, 0, GRID_W - NA_KW)
    col_in = (ck >= c0) & (ck < c0 + NA_KW)
    dx = jnp.clip(ck - cq, -(NA_KW - 1), NA_KW - 1) + (NA_KW - 1)
    for dy in range(n_dy):
        acc = jnp.zeros((GRID_W, GRID_W), F32)
        for d in range(n_dx):
            acc = jnp.where(dx == d, rpb_ref[(h * n_dy + dy) * n_dx + d], acc)
        o_ref[0, dy] = jnp.where(col_in, acc, NEG_INF)


def _bias_tiles(rpb):
    n_dy = 2 * NA_KH - 1
    return pl.pallas_call(
        _bias_body,
        grid=(HEADS,),
        in_specs=[pl.BlockSpec(memory_space=pltpu.SMEM)],
        out_specs=pl.BlockSpec((1, n_dy, GRID_W, GRID_W), lambda h: (h, 0, 0, 0)),
        out_shape=jax.ShapeDtypeStruct((HEADS, n_dy, GRID_W, GRID_W), F32),
        compiler_params=pltpu.CompilerParams(dimension_semantics=("parallel",)),
        name="bias",
    )(rpb.reshape(-1))


def _bias_variants(tiles):
    var = jnp.stack([tiles[:, v:v + NA_KH] for v in range(NA_KH)], axis=1)
    var = jnp.transpose(var, (0, 1, 3, 2, 4)).reshape(HEADS, NA_KH, GRID_W, NA_KH * GRID_W)
    var = var.reshape(HEADS // 2, 2, NA_KH, GRID_W, NA_KH * GRID_W)
    return jnp.transpose(var, (0, 2, 1, 3, 4)).reshape(HEADS // 2, NA_KH, 2 * GRID_W, NA_KH * GRID_W)


NA_ROWS_PER_STEP = 8


def _na_body(q_ref, k_ref, v_ref, kc_ref, vc_ref, bias_ref, o_ref, *, rows):
    j = pl.program_id(2)
    lane = lax.broadcasted_iota(jnp.int32, (GRID_W, 2 * HEAD_DIM), 1)
    first = lane < HEAD_DIM
    kc = kc_ref[0]
    vc = vc_ref[0]
    band = NA_KH * GRID_W
    for rr in range(NA_ROWS_PER_STEP):
        r = j * NA_ROWS_PER_STEP + rr
        r0 = jnp.clip(r - NA_KH // 2, 0, rows - NA_KH)
        variant = r0 - r + (NA_KH - 1)
        qr = q_ref[0, rr * GRID_W:(rr + 1) * GRID_W, :]
        zero = jnp.zeros_like(qr)
        lhs = jnp.concatenate([jnp.where(first, qr, zero), jnp.where(first, zero, qr)], axis=0)
        start = pl.multiple_of(r0 * GRID_W, GRID_W)
        kb = k_ref[0, pl.ds(start, band), :]
        vb = v_ref[0, pl.ds(start, band), :]
        s_win = _dot_nt(lhs, kb) + bias_ref[0, variant]
        s_ctx = _dot_nt(lhs, kc)
        m = jnp.maximum(jnp.max(s_win, axis=-1, keepdims=True), jnp.max(s_ctx, axis=-1, keepdims=True))
        p_win = jnp.exp(s_win - m)
        p_ctx = jnp.exp(s_ctx - m)
        denom = jnp.sum(p_win, axis=-1, keepdims=True) + jnp.sum(p_ctx, axis=-1, keepdims=True)
        o = (_dot(p_win.astype(BF16), vb) + _dot(p_ctx.astype(BF16), vc)) / denom
        o_ref[0, rr * GRID_W:(rr + 1) * GRID_W, :] = jnp.where(first, o[:GRID_W], o[GRID_W:]).astype(o_ref.dtype)


def _neighbourhood_attention(q, k, v, kc, vc, bias):
    b, t, _ = q.shape
    ctx_len = kc.shape[1]
    rows = t // GRID_W
    assert rows >= NA_KH and rows % NA_ROWS_PER_STEP == 0
    tq = NA_ROWS_PER_STEP * GRID_W
    pair_w = 2 * HEAD_DIM
    return pl.pallas_call(
        functools.partial(_na_body, rows=rows),
        grid=(b, HEADS // 2, t // tq),
        in_specs=[pl.BlockSpec((1, tq, pair_w), lambda bi, p, j: (bi, j, p)),
                  pl.BlockSpec((1, t, pair_w), lambda bi, p, j: (bi, 0, p)),
                  pl.BlockSpec((1, t, pair_w), lambda bi, p, j: (bi, 0, p)),
                  pl.BlockSpec((1, ctx_len, pair_w), lambda bi, p, j: (bi, 0, p)),
                  pl.BlockSpec((1, ctx_len, pair_w), lambda bi, p, j: (bi, 0, p)),
                  pl.BlockSpec((1,) + bias.shape[1:], lambda bi, p, j: (p, 0, 0, 0))],
        out_specs=pl.BlockSpec((1, tq, pair_w), lambda bi, p, j: (bi, j, p)),
        out_shape=jax.ShapeDtypeStruct((b, t, WIDTH), BF16),
        compiler_params=pltpu.CompilerParams(dimension_semantics=("parallel", "parallel", "arbitrary"),
                                             vmem_limit_bytes=VMEM_LIMIT),
        name="na",
    )(q, k, v, kc, vc, bias)


def _prep_body(x_ref, hp_ref, hn_ref, cw_ref, cos_ref, sina_ref, sinb_ref, e_ref, bg_ref, tril_ref, triu_ref,
               ex_ref, q_ref, k_ref, v_ref, bf_ref, bb_ref, gf_ref, gb_ref):
    i = pl.program_id(0)
    last = pl.num_programs(0) - 1
    x = x_ref[0]
    tc = x.shape[0]
    rowid = lax.broadcasted_iota(jnp.int32, x.shape, 0)
    before = jnp.where(i > 0, hp_ref[0, 0, 7:8, :], 0.0)
    after = jnp.where(i < last, hn_ref[0, 0, 0:1, :], 0.0)
    prev = jnp.where(rowid == 0, before, pltpu.roll(x, 1, axis=0))
    nxt = jnp.where(rowid == tc - 1, after, pltpu.roll(x, tc - 1, axis=0))
    y = _silu(prev * cw_ref[0:1, :] + x * cw_ref[1:2, :] + nxt * cw_ref[2:3, :])

    w = WIDTH
    half = HEAD_DIM // 4

    def norm_rope(a):
        a = a * lax.rsqrt(_seg_sum(a * a, e_ref) + EPS)
        return (a * cos_ref[...] + pltpu.roll(a, w - half, axis=1) * sina_ref[...]
                + pltpu.roll(a, half, axis=1) * sinb_ref[...])

    q_ref[0] = norm_rope(y[:, :w]) * HEAD_DIM ** -0.5
    k_ref[0] = norm_rope(y[:, w:2 * w])
    v_ref[0] = y[:, 2 * w:]

    bg = bg_ref[0]
    lane = lax.broadcasted_iota(jnp.int32, bg.shape, 1)
    cum_f = _dot(tril_ref[...], bg, precision=HIGHEST)
    cum_b = _dot(triu_ref[...], bg, precision=HIGHEST)
    src = jnp.where(lane < 2 * HEADS, bg, jnp.where(lane < 3 * HEADS, cum_f, cum_b))
    wide = _dot(src, ex_ref[...], precision=HIGHEST)
    bf_ref[0] = wide[:, :w]
    bb_ref[0] = wide[:, w:2 * w]
    gf_ref[0] = wide[:, 2 * w:3 * w]
    gb_ref[0] = wide[:, 3 * w:]


def _prepare(raw, bg, conv_w, cos, sina, sinb, e64, tril, triu, expand, tc):
    b, t, cw = raw.shape
    w = WIDTH
    nblk8 = t // 8
    raw4 = raw.reshape(b, nblk8, 8, cw)
    per8 = tc // 8
    tok = lambda width: pl.BlockSpec((1, tc, width), lambda i, bi: (bi, i, 0))
    tab = pl.BlockSpec((tc, w), lambda i, bi: (i, 0))
    full = lambda a: pl.BlockSpec(a.shape, lambda i, bi: (0,) * a.ndim)
    out = jax.ShapeDtypeStruct((b, t, w), F32)
    return pl.pallas_call(
        _prep_body,
        grid=(t // tc, b),
        in_specs=[tok(cw),
                  pl.BlockSpec((1, 1, 8, cw), lambda i, bi: (bi, jnp.maximum(i * per8 - 1, 0), 0, 0)),
                  pl.BlockSpec((1, 1, 8, cw), lambda i, bi: (bi, jnp.minimum((i + 1) * per8, nblk8 - 1), 0, 0)),
                  full(conv_w), tab, tab, tab, full(e64), tok(128), full(tril), full(triu), full(expand)],
        out_specs=[tok(w)] * 7,
        out_shape=[out] * 7,
        compiler_params=pltpu.CompilerParams(dimension_semantics=("parallel", "parallel"),
                                             vmem_limit_bytes=VMEM_LIMIT),
        name="prep",
    )(raw, raw4, raw4, conv_w, cos, sina, sinb, e64, bg, tril, triu, expand)


def _block_diag(x, mask):
    return jnp.concatenate([x] * GROUP, axis=0) * mask


def _delta_unit(q, k, v, beta, gc, state, dmat, bdmask, backward):
    if backward:
        incl, strict, last = dmat <= 0, dmat < 0, 0
    else:
        incl, strict, last = dmat >= 0, dmat > 0, CHUNK - 1
    eye = (dmat == 0).astype(F32)
    g_row = jnp.sum(gc * eye, axis=0, keepdims=True)
    decay = jnp.exp(jnp.where(incl, gc - g_row, NEG_INF))
    kb = k * beta
    k_bd = _block_diag(k, bdmask)
    both = _dot_nt(jnp.concatenate([kb, q], axis=0), k_bd)
    low = jnp.where(strict, both[:CHUNK] * decay, 0.0)
    qk = both[CHUNK:] * decay

    power = _dot(low, _block_diag(low, bdmask))
    inv = eye - low
    for _ in range(4):
        res = _dot(jnp.concatenate([power, inv], axis=0), _block_diag(power, bdmask))
        power, inv = res[:CHUNK], inv + res[CHUNK:]
    inv = inv + _dot(inv, _block_diag(power, bdmask))

    eg = jnp.exp(gc)
    g_last = gc[last:last + 1, :]
    u = _dot(inv, _block_diag(v * beta, bdmask))
    wk = _dot(inv, _block_diag(kb * eg, bdmask))
    q_dec = q * eg
    k_dec = k * jnp.exp(g_last - gc)
    g_tot = jnp.exp(g_last)

    proj = _dot(jnp.concatenate([wk, q_dec], axis=0), state)
    v_new = u - proj[:CHUNK]
    o = proj[CHUNK:] + _dot(qk, _block_diag(v_new, bdmask))
    new_state = state * g_tot + _dot_tn(k_dec, v_new) * bdmask
    return o, new_state


def _scan_body(qf_ref, kf_ref, vf_ref, bf_ref, gf_ref, qb_ref, kb_ref, vb_ref, bb_ref, gb_ref, s0_ref, dmat_ref,
               mask_ref, of_ref, ob_ref, s_ref):
    n = pl.program_id(1)

    @pl.when(n == 0)
    def _():
        s_ref[...] = s0_ref[...]

    dmat = dmat_ref[...]
    bdmask = mask_ref[...]
    dirs = ((qf_ref, kf_ref, vf_ref, bf_ref, gf_ref, of_ref), (qb_ref, kb_ref, vb_ref, bb_ref, gb_ref, ob_ref))
    for d, (q_ref, k_ref, v_ref, b_ref, g_ref, o_ref) in enumerate(dirs):
        for grp in range(HEADS // GROUP):
            sl = slice(grp * GW, (grp + 1) * GW)
            o, s_new = _delta_unit(q_ref[0, :, sl], k_ref[0, :, sl], v_ref[0, :, sl], b_ref[0, :, sl],
                                   g_ref[0, :, sl], s_ref[0, d, grp], dmat, bdmask, backward=bool(d))
            o_ref[0, :, sl] = o
            s_ref[0, d, grp] = s_new


def _delta_scan(q, k, v, beta_f, beta_b, gc_f, gc_b, s0, dmat, bdmask):
    b, t, w = q.shape
    n = t // CHUNK
    fwd = pl.BlockSpec((1, CHUNK, w), lambda bi, i: (bi, i, 0))
    bwd = pl.BlockSpec((1, CHUNK, w), lambda bi, i: (bi, n - 1 - i, 0))
    st = pl.BlockSpec((1,) + s0.shape[1:], lambda bi, i: (bi, 0, 0, 0, 0))
    full = lambda a: pl.BlockSpec(a.shape, lambda bi, i: (0, 0))
    o = jax.ShapeDtypeStruct((b, t, w), F32)
    return pl.pallas_call(
        _scan_body,
        grid=(b, n),
        in_specs=[fwd] * 5 + [bwd] * 5 + [st, full(dmat), full(bdmask)],
        out_specs=[fwd, bwd, st],
        out_shape=[o, o, jax.ShapeDtypeStruct(s0.shape, F32)],
        compiler_params=pltpu.CompilerParams(dimension_semantics=("parallel", "arbitrary"),
                                             vmem_limit_bytes=VMEM_LIMIT),
        name="scan",
    )(q, k, v, beta_f, gc_f, q, k, v, beta_b, gc_b, s0, dmat, bdmask)


def _merge_body(x_ref, ona_ref, gzna_ref, of_ref, ob_ref, gzdn_ref, sgna_ref, sgdn_ref, gate_ref, dnw_ref, e_ref,
                wna_ref, wdn_ref, wout_ref, o_ref):
    a = (ona_ref[0].astype(F32) * gzna_ref[0].astype(F32)).astype(BF16)
    u_na = _dot(a, wna_ref[...])
    od = of_ref[0] + ob_ref[0]
    odn = od * lax.rsqrt(_seg_sum(od * od, e_ref) * (1.0 / HEAD_DIM) + EPS) * dnw_ref[...]
    u_dn = _dot((odn * gzdn_ref[0].astype(F32)).astype(BF16), wdn_ref[...])
    y = sgna_ref[0].astype(F32) * u_na + sgdn_ref[0].astype(F32) * u_dn
    o_ref[0] = x_ref[0] + gate_ref[0] * _dot(y.astype(BF16), wout_ref[...])


def _merge(x, o_na, gz_na, o_f, o_b, gz_dn, sg_na, sg_dn, gate, dnw, e64, w_o_na, w_o_dn, w_out, tm):
    b, t, d = x.shape
    w = WIDTH
    tok = lambda width: pl.BlockSpec((1, tm, width), lambda bi, i: (bi, i, 0))
    full = lambda a: pl.BlockSpec(a.shape, lambda bi, i: (0, 0))
    return pl.pallas_call(
        _merge_body,
        grid=(b, t // tm),
        in_specs=[tok(d), tok(w), tok(w), tok(w), tok(w), tok(w), tok(d), tok(d),
                  pl.BlockSpec((1, 1, d), lambda bi, i: (bi, 0, 0)), full(dnw), full(e64),
                  full(w_o_na), full(w_o_dn), full(w_out)],
        out_specs=tok(d),
        out_shape=jax.ShapeDtypeStruct((b, t, d), F32),
        compiler_params=pltpu.CompilerParams(dimension_semantics=("parallel", "parallel"),
                                             vmem_limit_bytes=VMEM_LIMIT),
        name="merge",
    )(x, o_na, gz_na, o_f, o_b, gz_dn, sg_na, sg_dn, gate, dnw, e64, w_o_na, w_o_dn, w_out)


def _constants(tc):
    seg = np.arange(WIDTH) // HEAD_DIM
    e64 = (seg[:, None] == seg[None, :]).astype(np.float32)
    tok = np.arange(tc)
    same_chunk = (tok[:, None] // CHUNK) == (tok[None, :] // CHUNK)
    tril = (same_chunk & (tok[None, :] <= tok[:, None])).astype(np.float32)
    triu = (same_chunk & (tok[None, :] >= tok[:, None])).astype(np.float32)
    expand = np.zeros((128, 4 * WIDTH), np.float32)
    for s in range(4):
        for h in range(HEADS):
            expand[s * HEADS + h, s * WIDTH + h * HEAD_DIM:s * WIDTH + (h + 1) * HEAD_DIM] = 1.0
    lane = np.arange(GW)
    dmat = (np.arange(CHUNK)[:, None] - (lane % HEAD_DIM)[None, :]).astype(np.int32)
    bdmask = ((lane[:, None] // HEAD_DIM) == (lane[None, :] // HEAD_DIM)).astype(np.float32)
    return (jnp.asarray(e64, BF16), jnp.asarray(tril), jnp.asarray(triu), jnp.asarray(expand), jnp.asarray(dmat),
            jnp.asarray(bdmask))


def _rope_tables(t):
    half = HEAD_DIM // 4
    tok = jnp.arange(t)
    freqs = ROPE_BASE ** (-jnp.arange(half, dtype=F32) / half)
    ang_r = (tok // GRID_W).astype(F32)[:, None] * freqs[None, :]
    ang_c = (tok % GRID_W).astype(F32)[:, None] * freqs[None, :]
    cos = jnp.concatenate([jnp.cos(ang_r)] * 2 + [jnp.cos(ang_c)] * 2, axis=-1)
    sin = jnp.concatenate([jnp.sin(ang_r)] * 2 + [jnp.sin(ang_c)] * 2, axis=-1)
    lower = (jnp.arange(HEAD_DIM) % (2 * half)) < half
    sina = jnp.where(lower, -sin, 0.0)
    sinb = jnp.where(lower, 0.0, sin)
    tile = lambda a: jnp.tile(a, (1, HEADS))
    return tile(cos), tile(sina), tile(sinb)


def _pad_lanes(a, offset):
    return jnp.zeros((1, 128), F32).at[0, offset:offset + a.size].set(a.reshape(-1))


def _layer(x, ctx, c, c_ctx, mod_w, mod_b, norm_w, w_in, conv_w, na_q_norm, na_k_norm, na_rpb, dn_A_log, dn_dt_bias,
           dn_norm_w, w_o_na, w_o_dn, w_out):
    b, t, d = x.shape
    ctx_len = ctx.shape[1]
    w = WIDTH
    tm = 256
    tc = 256
    e64, tril, triu, expand, dmat, bdmask = _constants(tc)

    cc = jnp.zeros((8, d), F32).at[:b].set(c).at[b].set(c_ctx)
    mod = _modulation(cc, mod_w, mod_b)
    shift, scale, gate = mod[:, :d], mod[:, d:2 * d], mod[:, 2 * d:]
    rows_x = lambda a: a[:b, None, :]
    rows_c = lambda a: jnp.broadcast_to(a[b][None, None, :], (b, 1, d))

    n_ba = 4 * HEADS
    ba0 = 8 * w
    w_main = jnp.concatenate([w_in[:, :ba0], w_in[:, ba0 + n_ba:]], axis=1).astype(BF16)
    w_ba = jnp.zeros((d, 128), F32).at[:, :n_ba].set(w_in[:, ba0:ba0 + n_ba]).astype(BF16)
    nw = norm_w.reshape(1, d)
    qw = jnp.tile(na_q_norm, HEADS).reshape(1, w)
    kw = jnp.tile(na_k_norm, HEADS).reshape(1, w)
    alog = _pad_lanes(dn_A_log, 2 * HEADS)
    dtb = _pad_lanes(dn_dt_bias, 2 * HEADS)
    project = functools.partial(_project, nw=nw, w_main=w_main, w_ba=w_ba, qw=qw, kw=kw, e64=e64, alog=alog,
                                dtb=dtb, tm=tm)
    q_na, k_na, v_na, gz_na, dn_raw, gz_dn, bg, sg_na, sg_dn = project(x, rows_x(scale), rows_x(shift))
    _, k_c, v_c, _, dn_raw_c, _, bg_c, _, _ = project(ctx, rows_c(scale), rows_c(shift))

    bias = _bias_variants(_bias_tiles(na_rpb))
    o_na = _neighbourhood_attention(q_na, k_na, v_na, k_c, v_c, bias)

    prepare = functools.partial(_prepare, conv_w=conv_w, e64=e64, tril=tril, triu=triu, expand=expand, tc=tc)
    cos, sina, sinb = _rope_tables(t)
    one, zero = jnp.ones((ctx_len, w), F32), jnp.zeros((ctx_len, w), F32)
    pc = prepare(dn_raw_c, bg_c, cos=one, sina=zero, sinb=zero)
    px = prepare(dn_raw, bg, cos=cos, sina=sina, sinb=sinb)
    s_zero = jnp.zeros((b, 2, HEADS // GROUP, GW, GW), F32)
    _, _, s_ctx = _delta_scan(*pc, s_zero, dmat, bdmask)
    o_f, o_b, _ = _delta_scan(*px, s_ctx, dmat, bdmask)

    dnw = jnp.tile(dn_norm_w, HEADS).reshape(1, w)
    return _merge(x, o_na, gz_na, o_f, o_b, gz_dn, sg_na, sg_dn, rows_x(gate), dnw, e64,
                  w_o_na.astype(BF16), w_o_dn.astype(BF16), w_out.astype(BF16), tm)


def kernel(x, c, ctx, c_ctx, mod_w, mod_b, norm_w, w_in, conv_w, na_q_norm, na_k_norm, na_rpb, dn_A_log, dn_dt_bias,
           dn_norm_w, w_o_na, w_o_dn, w_out):
    depth = mod_w.shape[0]
    assert depth == 1, "context-stream update between layers is not implemented"
    return _layer(x, ctx, c, c_ctx, mod_w[0], mod_b[0], norm_w[0], w_in[0], conv_w[0], na_q_norm[0], na_k_norm[0],
                  na_rpb[0], dn_A_log[0], dn_dt_bias[0], dn_norm_w[0], w_o_na[0], w_o_dn[0], w_out[0])
```

```python
import functools

import jax
import jax.numpy as jnp
import numpy as np
from jax import lax
from jax.experimental import pallas as pl
from jax.experimental.pallas import tpu as pltpu

F32 = jnp.float32
BF16 = jnp.bfloat16
EPS = 1e-6
GRID_W = 64
HEADS = 8
HEAD_DIM = 64
WIDTH = HEADS * HEAD_DIM
NA_KH = 8
NA_KW = 16
CHUNK = 64
CONV_K = 3
ROPE_BASE = 10000.0
GROUP = 4
GW = GROUP * HEAD_DIM
NEG_INF = float("-inf")
HIGHEST = lax.Precision.HIGHEST
VMEM_LIMIT = 56 * 1024 * 1024


def _dot(a, b, **kw):
    return jnp.dot(a, b, preferred_element_type=F32, **kw)


def _dot_nt(a, b):
    return lax.dot_general(a, b, (((1,), (1,)), ((), ())), preferred_element_type=F32)


def _dot_tn(a, b):
    return lax.dot_general(a, b, (((0,), (0,)), ((), ())), preferred_element_type=F32)


def _seg_sum(x, e_ref):
    hi = x.astype(BF16)
    lo = (x - hi.astype(F32)).astype(BF16)
    e = e_ref[...]
    return _dot(hi, e) + _dot(lo, e)


def _silu(x):
    return x * jax.nn.sigmoid(x)


def _mod_body(c_ref, w_ref, b_ref, o_ref):
    o_ref[...] = _dot(_silu(c_ref[...]), w_ref[...], precision=HIGHEST) + b_ref[...]


def _modulation(cc, mod_w, mod_b):
    rows, d = cc.shape
    n = mod_w.shape[1]
    tn = 512
    return pl.pallas_call(
        _mod_body,
        grid=(n // tn,),
        in_specs=[pl.BlockSpec((rows, d), lambda j: (0, 0)),
                  pl.BlockSpec((d, tn), lambda j: (0, j)),
                  pl.BlockSpec((1, tn), lambda j: (0, j))],
        out_specs=pl.BlockSpec((rows, tn), lambda j: (0, j)),
        out_shape=jax.ShapeDtypeStruct((rows, n), F32),
        compiler_params=pltpu.CompilerParams(dimension_semantics=("parallel",)),
        name="mod",
    )(cc, mod_w, mod_b.reshape(1, n))


def _proj_body(x_ref, scale_ref, shift_ref, nw_ref, w_ref, wba_ref, qw_ref, kw_ref, e_ref, alog_ref, dtb_ref,
               q_ref, k_ref, v_ref, gzna_ref, dn_ref, gzdn_ref, bg_ref, sgna_ref, sgdn_ref):
    x = x_ref[0]
    xn = x * lax.rsqrt(jnp.mean(x * x, axis=-1, keepdims=True) + EPS)
    h = (xn * nw_ref[...]) * (1.0 + scale_ref[0]) + shift_ref[0]
    hb = h.astype(BF16)

    def mm(lo, hi):
        return _dot(hb, w_ref[:, lo:hi])

    def head_rms(a, w_row):
        return a * lax.rsqrt(_seg_sum(a * a, e_ref) * (1.0 / HEAD_DIM) + EPS) * w_row

    w = WIDTH
    q_ref[0] = (head_rms(mm(0, w), qw_ref[...]) * HEAD_DIM ** -0.5).astype(BF16)
    k_ref[0] = head_rms(mm(w, 2 * w), kw_ref[...]).astype(BF16)
    v_ref[0] = mm(2 * w, 3 * w).astype(BF16)
    gzna_ref[0] = _silu(mm(3 * w, 4 * w)).astype(BF16)
    dn_ref[0] = mm(4 * w, 7 * w)
    gzdn_ref[0] = _silu(mm(7 * w, 8 * w)).astype(BF16)
    sgna_ref[0] = jax.nn.sigmoid(mm(8 * w, 10 * w)).astype(BF16)
    sgdn_ref[0] = jax.nn.sigmoid(mm(10 * w, 12 * w)).astype(BF16)

    ba = _dot(hb, wba_ref[...])
    lane = lax.broadcasted_iota(jnp.int32, ba.shape, 1)
    a = ba + dtb_ref[...]
    softplus = jnp.maximum(a, 0.0) + jnp.log1p(jnp.exp(-jnp.abs(a)))
    g = -jnp.exp(alog_ref[...]) * softplus
    bg_ref[0] = jnp.where(lane < 2 * HEADS, jax.nn.sigmoid(ba), jnp.where(lane < 4 * HEADS, g, 0.0))


def _project(x, scale, shift, nw, w_main, w_ba, qw, kw, e64, alog, dtb, tm):
    b, t, d = x.shape
    w = WIDTH
    tok = lambda width: pl.BlockSpec((1, tm, width), lambda bi, i: (bi, i, 0))
    row = lambda width: pl.BlockSpec((1, width), lambda bi, i: (0, 0))
    per_batch = pl.BlockSpec((1, 1, d), lambda bi, i: (bi, 0, 0))
    full = lambda a: pl.BlockSpec(a.shape, lambda bi, i: (0, 0))
    sds = lambda width, dt: jax.ShapeDtypeStruct((b, t, width), dt)
    return pl.pallas_call(
        _proj_body,
        grid=(b, t // tm),
        in_specs=[tok(d), per_batch, per_batch, row(d), full(w_main), full(w_ba), row(w), row(w), full(e64),
                  row(128), row(128)],
        out_specs=[tok(w), tok(w), tok(w), tok(w), tok(3 * w), tok(w), tok(128), tok(2 * w), tok(2 * w)],
        out_shape=[sds(w, BF16), sds(w, BF16), sds(w, BF16), sds(w, BF16), sds(3 * w, F32), sds(w, BF16),
                   sds(128, F32), sds(2 * w, BF16), sds(2 * w, BF16)],
        compiler_params=pltpu.CompilerParams(dimension_semantics=("parallel", "parallel"),
                                             vmem_limit_bytes=VMEM_LIMIT),
        name="proj",
    )(x, scale, shift, nw, w_main, w_ba, qw, kw, e64, alog, dtb)


def _bias_body(rpb_ref, o_ref):
    h = pl.program_id(0)
    n_dy, n_dx = 2 * NA_KH - 1, 2 * NA_KW - 1
    cq = lax.broadcasted_iota(jnp.int32, (GRID_W, GRID_W), 0)
    ck = lax.broadcasted_iota(jnp.int32, (GRID_W, GRID_W), 1)
    c0 = jnp.clip(cq - NA_KW // 2, 0, GRID_W - NA_KW)
    col_in = (ck >= c0) & (ck < c0 + NA_KW)
    dx = jnp.clip(ck - cq, -(NA_KW - 1), NA_KW - 1) + (NA_KW - 1)
    for dy in range(n_dy):
        acc = jnp.zeros((GRID_W, GRID_W), F32)
        for d in range(n_dx):
            acc = jnp.where(dx == d, rpb_ref[(h * n_dy + dy) * n_dx + d], acc)
        o_ref[0, dy] = jnp.where(col_in, acc, NEG_INF)


def _bias_tiles(rpb):
    n_dy = 2 * NA_KH - 1
    return pl.pallas_call(
        _bias_body,
        grid=(HEADS,),
        in_specs=[pl.BlockSpec(memory_space=pltpu.SMEM)],
        out_specs=pl.BlockSpec((1, n_dy, GRID_W, GRID_W), lambda h: (h, 0, 0, 0)),
        out_shape=jax.ShapeDtypeStruct((HEADS, n_dy, GRID_W, GRID_W), F32),
        compiler_params=pltpu.CompilerParams(dimension_semantics=("parallel",)),
        name="bias",
    )(rpb.reshape(-1))


def _bias_variants(tiles):
    var = jnp.stack([tiles[:, v:v + NA_KH] for v in range(NA_KH)], axis=1)
    var = jnp.transpose(var, (0, 1, 3, 2, 4)).reshape(HEADS, NA_KH, GRID_W, NA_KH * GRID_W)
    var = var.reshape(HEADS // 2, 2, NA_KH, GRID_W, NA_KH * GRID_W)
    return jnp.transpose(var, (0, 2, 1, 3, 4)).reshape(HEADS // 2, NA_KH, 2 * GRID_W, NA_KH * GRID_W)


NA_ROWS_PER_STEP = 8


def _na_body(q_ref, k_ref, v_ref, kc_ref, vc_ref, bias_ref, o_ref, *, rows):
    j = pl.program_id(2)
    lane = lax.broadcasted_iota(jnp.int32, (GRID_W, 2 * HEAD_DIM), 1)
    first = lane < HEAD_DIM
    kc = kc_ref[0]
    vc = vc_ref[0]
    band = NA_KH * GRID_W
    for rr in range(NA_ROWS_PER_STEP):
        r = j * NA_ROWS_PER_STEP + rr
        r0 = jnp.clip(r - NA_KH // 2, 0, rows - NA_KH)
        variant = r0 - r + (NA_KH - 1)
        qr = q_ref[0, rr * GRID_W:(rr + 1) * GRID_W, :]
        zero = jnp.zeros_like(qr)
        lhs = jnp.concatenate([jnp.where(first, qr, zero), jnp.where(first, zero, qr)], axis=0)
        start = pl.multiple_of(r0 * GRID_W, GRID_W)
        kb = k_ref[0, pl.ds(start, band), :]
        vb = v_ref[0, pl.ds(start, band), :]
        s_win = _dot_nt(lhs, kb) + bias_ref[0, variant]
        s_ctx = _dot_nt(lhs, kc)
        m = jnp.maximum(jnp.max(s_win, axis=-1, keepdims=True), jnp.max(s_ctx, axis=-1, keepdims=True))
        p_win = jnp.exp(s_win - m)
        p_ctx = jnp.exp(s_ctx - m)
        denom = jnp.sum(p_win, axis=-1, keepdims=True) + jnp.sum(p_ctx, axis=-1, keepdims=True)
        o = (_dot(p_win.astype(BF16), vb) + _dot(p_ctx.astype(BF16), vc)) / denom
        o_ref[0, rr * GRID_W:(rr + 1) * GRID_W, :] = jnp.where(first, o[:GRID_W], o[GRID_W:]).astype(o_ref.dtype)


def _neighbourhood_attention(q, k, v, kc, vc, bias):
    b, t, _ = q.shape
    ctx_len = kc.shape[1]
    rows = t // GRID_W
    assert rows >= NA_KH and rows % NA_ROWS_PER_STEP == 0
    tq = NA_ROWS_PER_STEP * GRID_W
    pair_w = 2 * HEAD_DIM
    return pl.pallas_call(
        functools.partial(_na_body, rows=rows),
        grid=(b, HEADS // 2, t // tq),
        in_specs=[pl.BlockSpec((1, tq, pair_w), lambda bi, p, j: (bi, j, p)),
                  pl.BlockSpec((1, t, pair_w), lambda bi, p, j: (bi, 0, p)),
                  pl.BlockSpec((1, t, pair_w), lambda bi, p, j: (bi, 0, p)),
                  pl.BlockSpec((1, ctx_len, pair_w), lambda bi, p, j: (bi, 0, p)),
                  pl.BlockSpec((1, ctx_len, pair_w), lambda bi, p, j: (bi, 0, p)),
                  pl.BlockSpec((1,) + bias.shape[1:], lambda bi, p, j: (p, 0, 0, 0))],
        out_specs=pl.BlockSpec((1, tq, pair_w), lambda bi, p, j: (bi, j, p)),
        out_shape=jax.ShapeDtypeStruct((b, t, WIDTH), BF16),
        compiler_params=pltpu.CompilerParams(dimension_semantics=("parallel", "parallel", "arbitrary"),
                                             vmem_limit_bytes=VMEM_LIMIT),
        name="na",
    )(q, k, v, kc, vc, bias)


def _prep_body(x_ref, hp_ref, hn_ref, cw_ref, cos_ref, sina_ref, sinb_ref, e_ref, bg_ref, tril_ref, triu_ref,
               ex_ref, q_ref, k_ref, v_ref, bf_ref, bb_ref, gf_ref, gb_ref):
    i = pl.program_id(0)
    last = pl.num_programs(0) - 1
    x = x_ref[0]
    tc = x.shape[0]
    rowid = lax.broadcasted_iota(jnp.int32, x.shape, 0)
    before = jnp.where(i > 0, hp_ref[0, 0, 7:8, :], 0.0)
    after = jnp.where(i < last, hn_ref[0, 0, 0:1, :], 0.0)
    prev = jnp.where(rowid == 0, before, pltpu.roll(x, 1, axis=0))
    nxt = jnp.where(rowid == tc - 1, after, pltpu.roll(x, tc - 1, axis=0))
    y = _silu(prev * cw_ref[0:1, :] + x * cw_ref[1:2, :] + nxt * cw_ref[2:3, :])

    w = WIDTH
    half = HEAD_DIM // 4

    def norm_rope(a):
        a = a * lax.rsqrt(_seg_sum(a * a, e_ref) + EPS)
        return (a * cos_ref[...] + pltpu.roll(a, w - half, axis=1) * sina_ref[...]
                + pltpu.roll(a, half, axis=1) * sinb_ref[...])

    q_ref[0] = norm_rope(y[:, :w]) * HEAD_DIM ** -0.5
    k_ref[0] = norm_rope(y[:, w:2 * w])
    v_ref[0] = y[:, 2 * w:]

    bg = bg_ref[0]
    lane = lax.broadcasted_iota(jnp.int32, bg.shape, 1)
    cum_f = _dot(tril_ref[...], bg, precision=HIGHEST)
    cum_b = _dot(triu_ref[...], bg, precision=HIGHEST)
    src = jnp.where(lane < 2 * HEADS, bg, jnp.where(lane < 3 * HEADS, cum_f, cum_b))
    wide = _dot(src, ex_ref[...], precision=HIGHEST)
    bf_ref[0] = wide[:, :w]
    bb_ref[0] = wide[:, w:2 * w]
    gf_ref[0] = wide[:, 2 * w:3 * w]
    gb_ref[0] = wide[:, 3 * w:]


def _prepare(raw, bg, conv_w, cos, sina, sinb, e64, tril, triu, expand, tc):
    b, t, cw = raw.shape
    w = WIDTH
    nblk8 = t // 8
    raw4 = raw.reshape(b, nblk8, 8, cw)
    per8 = tc // 8
    tok = lambda width: pl.BlockSpec((1, tc, width), lambda i, bi: (bi, i, 0))
    tab = pl.BlockSpec((tc, w), lambda i, bi: (i, 0))
    full = lambda a: pl.BlockSpec(a.shape, lambda i, bi: (0,) * a.ndim)
    out = jax.ShapeDtypeStruct((b, t, w), F32)
    return pl.pallas_call(
        _prep_body,
        grid=(t // tc, b),
        in_specs=[tok(cw),
                  pl.BlockSpec((1, 1, 8, cw), lambda i, bi: (bi, jnp.maximum(i * per8 - 1, 0), 0, 0)),
                  pl.BlockSpec((1, 1, 8, cw), lambda i, bi: (bi, jnp.minimum((i + 1) * per8, nblk8 - 1), 0, 0)),
                  full(conv_w), tab, tab, tab, full(e64), tok(128), full(tril), full(triu), full(expand)],
        out_specs=[tok(w)] * 7,
        out_shape=[out] * 7,
        compiler_params=pltpu.CompilerParams(dimension_semantics=("parallel", "parallel"),
                                             vmem_limit_bytes=VMEM_LIMIT),
        name="prep",
    )(raw, raw4, raw4, conv_w, cos, sina, sinb, e64, bg, tril, triu, expand)


def _block_diag(x, mask):
    return jnp.concatenate([x.astype(BF16)] * GROUP, axis=0) * mask


def _bdot(a, b):
    return _dot(a.astype(BF16), b.astype(BF16))


def _chunk_body(q_ref, k_ref, v_ref, bf_ref, bb_ref, gf_ref, gb_ref, dmat_ref, mask_ref, *out_refs, chunks):
    dmat = dmat_ref[...]
    bdmask = mask_ref[...]
    eye = (dmat == 0).astype(F32)
    n_out = len(out_refs) // 2
    bd = lambda x: _block_diag(x, bdmask)
    tile = lambda ref, c, grp: ref[0, c * CHUNK:(c + 1) * CHUNK, grp * GW:(grp + 1) * GW]
    tiles = [(c, grp) for c in range(chunks) for grp in range(HEADS // GROUP)]
    units = [(c, grp, d) for c, grp in tiles for d in range(2)]

    both = {t: _dot_nt(jnp.concatenate([tile(k_ref, *t), tile(q_ref, *t)], axis=0).astype(BF16), bd(tile(k_ref, *t)))
            for t in tiles}

    row = lax.broadcasted_iota(jnp.int32, dmat.shape, 0)
    col = row - dmat
    same16, same32 = (row // 16) == (col // 16), (row // 32) == (col // 32)
    qk, inv, diag, off32, off64 = {}, {}, {}, {}, {}
    for c, grp, d in units:
        beta, gc = tile((bf_ref, bb_ref)[d], c, grp), tile((gf_ref, gb_ref)[d], c, grp)
        incl, strict = (dmat <= 0, dmat < 0) if d else (dmat >= 0, dmat > 0)
        g_row = jnp.sum(gc * eye, axis=0, keepdims=True)
        decay = jnp.exp(jnp.where(incl, gc - g_row, NEG_INF))
        kk_qk = both[c, grp]
        low = jnp.where(strict, kk_qk[:CHUNK] * beta * decay, 0.0)
        qk[c, grp, d] = kk_qk[CHUNK:] * decay
        diag[c, grp, d] = jnp.where(same16, low, 0.0)
        off32[c, grp, d] = jnp.where(same32 & ~same16, low, 0.0)
        off64[c, grp, d] = jnp.where(same32, 0.0, low)
        inv[c, grp, d] = eye - diag[c, grp, d]

    power = {un: _bdot(diag[un], bd(diag[un])) for un in units}
    for _ in range(2):
        res = {un: _bdot(jnp.concatenate([power[un], inv[un]], axis=0), bd(power[un])) for un in units}
        power = {un: res[un][:CHUNK] for un in units}
        inv = {un: inv[un] + res[un][CHUNK:] for un in units}
    inv = {un: inv[un] + _bdot(inv[un], bd(power[un])) for un in units}
    for off in (off32, off64):
        half = {un: _bdot(inv[un], bd(off[un])) for un in units}
        inv = {un: inv[un] - _bdot(half[un], bd(inv[un])) for un in units}

    for c, grp, d in units:
        un = (c, grp, d)
        q, k, v = tile(q_ref, c, grp), tile(k_ref, c, grp), tile(v_ref, c, grp)
        beta, gc = tile((bf_ref, bb_ref)[d], c, grp), tile((gf_ref, gb_ref)[d], c, grp)
        last = 0 if d else CHUNK - 1
        eg = jnp.exp(gc)
        g_last = gc[last:last + 1, :]
        u = _bdot(inv[un], bd(v * beta))
        wk = _bdot(inv[un], bd(k * beta * eg))
        k_dec = k * jnp.exp(g_last - gc)
        u_ref, wk_ref, qk_ref, qd_ref, kdt_ref, gt_ref = out_refs[d * n_out:(d + 1) * n_out]
        rows, sl = slice(c * CHUNK, (c + 1) * CHUNK), slice(grp * GW, (grp + 1) * GW)
        u_ref[0, rows, sl] = u.astype(BF16)
        wk_ref[0, rows, sl] = wk.astype(BF16)
        qk_ref[0, rows, sl] = qk[un].astype(BF16)
        qd_ref[0, rows, sl] = (q * eg).astype(BF16)
        kdt_ref[0, c * WIDTH + grp * GW:c * WIDTH + (grp + 1) * GW, :] = k_dec.T.astype(BF16)
        gt_ref[0, c, :, sl] = jnp.exp(g_last)


def _chunk_terms(q, k, v, beta_f, beta_b, gc_f, gc_b, dmat, bdmask, chunks):
    b, t, w = q.shape
    n = t // CHUNK
    tc = chunks * CHUNK
    tok = pl.BlockSpec((1, tc, w), lambda bi, i: (bi, i, 0))
    full = lambda a: pl.BlockSpec(a.shape, lambda bi, i: (0, 0))
    per_dir_specs = [tok] * 4 + [pl.BlockSpec((1, chunks * w, CHUNK), lambda bi, i: (bi, i, 0)),
                                 pl.BlockSpec((1, chunks, 1, w), lambda bi, i: (bi, i, 0, 0))]
    act = jax.ShapeDtypeStruct((b, t, w), BF16)
    per_dir_shapes = [act] * 4 + [jax.ShapeDtypeStruct((b, n * w, CHUNK), BF16),
                                  jax.ShapeDtypeStruct((b, n, 1, w), F32)]
    return pl.pallas_call(
        functools.partial(_chunk_body, chunks=chunks),
        grid=(b, t // tc),
        in_specs=[tok] * 7 + [full(dmat), full(bdmask)],
        out_specs=per_dir_specs * 2,
        out_shape=per_dir_shapes * 2,
        compiler_params=pltpu.CompilerParams(dimension_semantics=("parallel", "parallel"),
                                             vmem_limit_bytes=VMEM_LIMIT),
        name="chunk",
    )(q, k, v, beta_f, beta_b, gc_f, gc_b, dmat, bdmask)


def _seq_body(*refs, batch):
    fwd, bwd = refs[0:6], refs[6:12]
    s0_ref, mask_ref, of_ref, ob_ref, s_ref = refs[12:]

    @pl.when(pl.program_id(0) == 0)
    def _():
        s_ref[...] = s0_ref[...]

    bdmask = mask_ref[...]
    keep = bdmask.astype(F32)
    dirs = ((fwd, of_ref), (bwd, ob_ref))
    units = [(b, d, grp) for b in range(batch) for d in range(2) for grp in range(HEADS // GROUP)]
    lanes = lambda grp: slice(grp * GW, (grp + 1) * GW)

    proj = {}
    for b, d, grp in units:
        _, wk_ref, _, qd_ref, _, _ = dirs[d][0]
        lhs = jnp.concatenate([wk_ref[b, :, lanes(grp)], qd_ref[b, :, lanes(grp)]], axis=0)
        proj[b, d, grp] = _bdot(lhs, s_ref[b, d, grp])
    v_new = {}
    for b, d, grp in units:
        v_new[b, d, grp] = dirs[d][0][0][b, :, lanes(grp)].astype(F32) - proj[b, d, grp][:CHUNK]
    for b, d, grp in units:
        (_, _, qk_ref, _, kdt_ref, gt_ref), o_ref = dirs[d]
        vn = v_new[b, d, grp]
        update = _bdot(kdt_ref[b, lanes(grp), :], vn)
        s_ref[b, d, grp] = s_ref[b, d, grp] * gt_ref[b, 0, :, lanes(grp)] + update * keep
        o_ref[b, :, lanes(grp)] = proj[b, d, grp][CHUNK:] + _bdot(qk_ref[b, :, lanes(grp)], _block_diag(vn, bdmask))


def _delta_scan(terms, s0, bdmask):
    b, t, w = terms[0].shape
    n = t // CHUNK

    def specs(idx):
        return ([pl.BlockSpec((b, CHUNK, w), lambda i: (0, idx(i), 0))] * 4
                + [pl.BlockSpec((b, w, CHUNK), lambda i: (0, idx(i), 0)),
                   pl.BlockSpec((b, 1, 1, w), lambda i: (0, idx(i), 0, 0))])

    forward, backward = (lambda i: i), (lambda i: n - 1 - i)
    st = pl.BlockSpec(s0.shape, lambda i: (0,) * s0.ndim)
    o = jax.ShapeDtypeStruct((b, t, w), F32)
    return pl.pallas_call(
        functools.partial(_seq_body, batch=b),
        grid=(n,),
        in_specs=specs(forward) + specs(backward) + [st, pl.BlockSpec(bdmask.shape, lambda i: (0, 0))],
        out_specs=[specs(forward)[0], specs(backward)[0], st],
        out_shape=[o, o, jax.ShapeDtypeStruct(s0.shape, F32)],
        compiler_params=pltpu.CompilerParams(dimension_semantics=("arbitrary",), vmem_limit_bytes=VMEM_LIMIT),
        name="seq",
    )(*terms, s0, bdmask)


def _merge_body(x_ref, ona_ref, gzna_ref, of_ref, ob_ref, gzdn_ref, sgna_ref, sgdn_ref, gate_ref, dnw_ref, e_ref,
                wna_ref, wdn_ref, wout_ref, o_ref):
    a = (ona_ref[0].astype(F32) * gzna_ref[0].astype(F32)).astype(BF16)
    u_na = _dot(a, wna_ref[...])
    od = of_ref[0] + ob_ref[0]
    odn = od * lax.rsqrt(_seg_sum(od * od, e_ref) * (1.0 / HEAD_DIM) + EPS) * dnw_ref[...]
    u_dn = _dot((odn * gzdn_ref[0].astype(F32)).astype(BF16), wdn_ref[...])
    y = sgna_ref[0].astype(F32) * u_na + sgdn_ref[0].astype(F32) * u_dn
    o_ref[0] = x_ref[0] + gate_ref[0] * _dot(y.astype(BF16), wout_ref[...])


def _merge(x, o_na, gz_na, o_f, o_b, gz_dn, sg_na, sg_dn, gate, dnw, e64, w_o_na, w_o_dn, w_out, tm):
    b, t, d = x.shape
    w = WIDTH
    tok = lambda width: pl.BlockSpec((1, tm, width), lambda bi, i: (bi, i, 0))
    full = lambda a: pl.BlockSpec(a.shape, lambda bi, i: (0, 0))
    return pl.pallas_call(
        _merge_body,
        grid=(b, t // tm),
        in_specs=[tok(d), tok(w), tok(w), tok(w), tok(w), tok(w), tok(d), tok(d),
                  pl.BlockSpec((1, 1, d), lambda bi, i: (bi, 0, 0)), full(dnw), full(e64),
                  full(w_o_na), full(w_o_dn), full(w_out)],
        out_specs=tok(d),
        out_shape=jax.ShapeDtypeStruct((b, t, d), F32),
        compiler_params=pltpu.CompilerParams(dimension_semantics=("parallel", "parallel"),
                                             vmem_limit_bytes=VMEM_LIMIT),
        name="merge",
    )(x, o_na, gz_na, o_f, o_b, gz_dn, sg_na, sg_dn, gate, dnw, e64, w_o_na, w_o_dn, w_out)


def _constants(tc):
    seg = np.arange(WIDTH) // HEAD_DIM
    e64 = (seg[:, None] == seg[None, :]).astype(np.float32)
    tok = np.arange(tc)
    same_chunk = (tok[:, None] // CHUNK) == (tok[None, :] // CHUNK)
    tril = (same_chunk & (tok[None, :] <= tok[:, None])).astype(np.float32)
    triu = (same_chunk & (tok[None, :] >= tok[:, None])).astype(np.float32)
    expand = np.zeros((128, 4 * WIDTH), np.float32)
    for s in range(4):
        for h in range(HEADS):
            expand[s * HEADS + h, s * WIDTH + h * HEAD_DIM:s * WIDTH + (h + 1) * HEAD_DIM] = 1.0
    lane = np.arange(GW)
    dmat = (np.arange(CHUNK)[:, None] - (lane % HEAD_DIM)[None, :]).astype(np.int32)
    bdmask = ((lane[:, None] // HEAD_DIM) == (lane[None, :] // HEAD_DIM)).astype(np.float32)
    return (jnp.asarray(e64, BF16), jnp.asarray(tril), jnp.asarray(triu), jnp.asarray(expand), jnp.asarray(dmat),
            jnp.asarray(bdmask, BF16))


def _rope_tables(t):
    half = HEAD_DIM // 4
    tok = jnp.arange(t)
    freqs = ROPE_BASE ** (-jnp.arange(half, dtype=F32) / half)
    ang_r = (tok // GRID_W).astype(F32)[:, None] * freqs[None, :]
    ang_c = (tok % GRID_W).astype(F32)[:, None] * freqs[None, :]
    cos = jnp.concatenate([jnp.cos(ang_r)] * 2 + [jnp.cos(ang_c)] * 2, axis=-1)
    sin = jnp.concatenate([jnp.sin(ang_r)] * 2 + [jnp.sin(ang_c)] * 2, axis=-1)
    lower = (jnp.arange(HEAD_DIM) % (2 * half)) < half
    sina = jnp.where(lower, -sin, 0.0)
    sinb = jnp.where(lower, 0.0, sin)
    tile = lambda a: jnp.tile(a, (1, HEADS))
    return tile(cos), tile(sina), tile(sinb)


def _pad_lanes(a, offset):
    return jnp.zeros((1, 128), F32).at[0, offset:offset + a.size].set(a.reshape(-1))


def _layer(x, ctx, c, c_ctx, mod_w, mod_b, norm_w, w_in, conv_w, na_q_norm, na_k_norm, na_rpb, dn_A_log, dn_dt_bias,
           dn_norm_w, w_o_na, w_o_dn, w_out):
    b, t, d = x.shape
    ctx_len = ctx.shape[1]
    w = WIDTH
    tm = 256
    tc = 256
    e64, tril, triu, expand, dmat, bdmask = _constants(tc)

    cc = jnp.zeros((8, d), F32).at[:b].set(c).at[b].set(c_ctx)
    mod = _modulation(cc, mod_w, mod_b)
    shift, scale, gate = mod[:, :d], mod[:, d:2 * d], mod[:, 2 * d:]
    rows_x = lambda a: a[:b, None, :]
    rows_c = lambda a: jnp.broadcast_to(a[b][None, None, :], (b, 1, d))

    n_ba = 4 * HEADS
    ba0 = 8 * w
    w_main = jnp.concatenate([w_in[:, :ba0], w_in[:, ba0 + n_ba:]], axis=1).astype(BF16)
    w_ba = jnp.zeros((d, 128), F32).at[:, :n_ba].set(w_in[:, ba0:ba0 + n_ba]).astype(BF16)
    nw = norm_w.reshape(1, d)
    qw = jnp.tile(na_q_norm, HEADS).reshape(1, w)
    kw = jnp.tile(na_k_norm, HEADS).reshape(1, w)
    alog = _pad_lanes(dn_A_log, 2 * HEADS)
    dtb = _pad_lanes(dn_dt_bias, 2 * HEADS)
    project = functools.partial(_project, nw=nw, w_main=w_main, w_ba=w_ba, qw=qw, kw=kw, e64=e64, alog=alog,
                                dtb=dtb, tm=tm)
    q_na, k_na, v_na, gz_na, dn_raw, gz_dn, bg, sg_na, sg_dn = project(x, rows_x(scale), rows_x(shift))
    _, k_c, v_c, _, dn_raw_c, _, bg_c, _, _ = project(ctx, rows_c(scale), rows_c(shift))

    bias = _bias_variants(_bias_tiles(na_rpb))
    o_na = _neighbourhood_attention(q_na, k_na, v_na, k_c, v_c, bias)

    prepare = functools.partial(_prepare, conv_w=conv_w, e64=e64, tril=tril, triu=triu, expand=expand, tc=tc)
    cos, sina, sinb = _rope_tables(t)
    one, zero = jnp.ones((ctx_len, w), F32), jnp.zeros((ctx_len, w), F32)
    pc = prepare(dn_raw_c, bg_c, cos=one, sina=zero, sinb=zero)
    px = prepare(dn_raw, bg, cos=cos, sina=sina, sinb=sinb)
    s_zero = jnp.zeros((b, 2, HEADS // GROUP, GW, GW), F32)
    _, _, s_ctx = _delta_scan(_chunk_terms(*pc, dmat, bdmask, chunks=2), s_zero, bdmask)
    o_f, o_b, _ = _delta_scan(_chunk_terms(*px, dmat, bdmask, chunks=2), s_ctx, bdmask)

    dnw = jnp.tile(dn_norm_w, HEADS).reshape(1, w)
    return _merge(x, o_na, gz_na, o_f, o_b, gz_dn, sg_na, sg_dn, rows_x(gate), dnw, e64,
                  w_o_na.astype(BF16), w_o_dn.astype(BF16), w_out.astype(BF16), tm)


def kernel(x, c, ctx, c_ctx, mod_w, mod_b, norm_w, w_in, conv_w, na_q_norm, na_k_norm, na_rpb, dn_A_log, dn_dt_bias,
           dn_norm_w, w_o_na, w_o_dn, w_out):
    depth = mod_w.shape[0]
    assert depth == 1, "context-stream update between layers is not implemented"
    return _layer(x, ctx, c, c_ctx, mod_w[0], mod_b[0], norm_w[0], w_in[0], conv_w[0], na_q_norm[0], na_k_norm[0],
                  na_rpb[0], dn_A_log[0], dn_dt_bias[0], dn_norm_w[0], w_o_na[0], w_o_dn[0], w_out[0])
```

```python
import functools

import jax
import jax.numpy as jnp
import numpy as np
from jax import lax
from jax.experimental import pallas as pl
from jax.experimental.pallas import tpu as pltpu

F32 = jnp.float32
BF16 = jnp.bfloat16
EPS = 1e-6
GRID_W = 64
HEADS = 8
HEAD_DIM = 64
WIDTH = HEADS * HEAD_DIM
NA_KH = 8
NA_KW = 16
CHUNK = 64
CONV_K = 3
ROPE_BASE = 10000.0
GROUP = 4
GW = GROUP * HEAD_DIM
NEG_INF = float("-inf")
HIGHEST = lax.Precision.HIGHEST
VMEM_LIMIT = 56 * 1024 * 1024


def _dot(a, b, **kw):
    return jnp.dot(a, b, preferred_element_type=F32, **kw)


def _dot_nt(a, b):
    return lax.dot_general(a, b, (((1,), (1,)), ((), ())), preferred_element_type=F32)


def _dot_tn(a, b):
    return lax.dot_general(a, b, (((0,), (0,)), ((), ())), preferred_element_type=F32)


def _seg_sum(x, e_ref):
    hi = x.astype(BF16)
    lo = (x - hi.astype(F32)).astype(BF16)
    e = e_ref[...]
    return _dot(hi, e) + _dot(lo, e)


def _silu(x):
    return x * jax.nn.sigmoid(x)


def _mod_body(c_ref, w_ref, b_ref, o_ref):
    o_ref[...] = _dot(_silu(c_ref[...]), w_ref[...], precision=HIGHEST) + b_ref[...]


def _modulation(cc, mod_w, mod_b):
    rows, d = cc.shape
    n = mod_w.shape[1]
    tn = 512
    return pl.pallas_call(
        _mod_body,
        grid=(n // tn,),
        in_specs=[pl.BlockSpec((rows, d), lambda j: (0, 0)),
                  pl.BlockSpec((d, tn), lambda j: (0, j)),
                  pl.BlockSpec((1, tn), lambda j: (0, j))],
        out_specs=pl.BlockSpec((rows, tn), lambda j: (0, j)),
        out_shape=jax.ShapeDtypeStruct((rows, n), F32),
        compiler_params=pltpu.CompilerParams(dimension_semantics=("parallel",)),
        name="mod",
    )(cc, mod_w, mod_b.reshape(1, n))


def _proj_body(x_ref, scale_ref, shift_ref, nw_ref, w_ref, wba_ref, qw_ref, kw_ref, e_ref, alog_ref, dtb_ref,
               q_ref, kt_ref, v_ref, gzna_ref, dn_ref, gzdn_ref, bg_ref, sgna_ref, sgdn_ref):
    x = x_ref[0]
    xn = x * lax.rsqrt(jnp.mean(x * x, axis=-1, keepdims=True) + EPS)
    h = (xn * nw_ref[...]) * (1.0 + scale_ref[0]) + shift_ref[0]
    hb = h.astype(BF16)

    def mm(lo, hi):
        return _dot(hb, w_ref[:, lo:hi])

    def head_rms(a, w_row):
        return a * lax.rsqrt(_seg_sum(a * a, e_ref) * (1.0 / HEAD_DIM) + EPS) * w_row

    w = WIDTH
    q_ref[0] = (head_rms(mm(0, w), qw_ref[...]) * HEAD_DIM ** -0.5).astype(BF16)
    k_t = head_rms(mm(w, 2 * w), kw_ref[...]).T.astype(BF16)
    for i in range(kt_ref.shape[1]):
        kt_ref[0, i] = k_t[:, i * 128:(i + 1) * 128]
    v_ref[0] = mm(2 * w, 3 * w).astype(BF16)
    gzna_ref[0] = _silu(mm(3 * w, 4 * w)).astype(BF16)
    dn_ref[0] = mm(4 * w, 7 * w)
    gzdn_ref[0] = _silu(mm(7 * w, 8 * w)).astype(BF16)
    sgna_ref[0] = jax.nn.sigmoid(mm(8 * w, 10 * w)).astype(BF16)
    sgdn_ref[0] = jax.nn.sigmoid(mm(10 * w, 12 * w)).astype(BF16)

    ba = _dot(hb, wba_ref[...])
    lane = lax.broadcasted_iota(jnp.int32, ba.shape, 1)
    a = ba + dtb_ref[...]
    softplus = jnp.maximum(a, 0.0) + jnp.log1p(jnp.exp(-jnp.abs(a)))
    g = -jnp.exp(alog_ref[...]) * softplus
    bg_ref[0] = jnp.where(lane < 2 * HEADS, jax.nn.sigmoid(ba), jnp.where(lane < 4 * HEADS, g, 0.0))


def _project(x, scale, shift, nw, w_main, w_ba, qw, kw, e64, alog, dtb, tm):
    b, t, d = x.shape
    w = WIDTH
    tok = lambda width: pl.BlockSpec((1, tm, width), lambda bi, i: (bi, i, 0))
    row = lambda width: pl.BlockSpec((1, width), lambda bi, i: (0, 0))
    per_batch = pl.BlockSpec((1, 1, d), lambda bi, i: (bi, 0, 0))
    full = lambda a: pl.BlockSpec(a.shape, lambda bi, i: (0, 0))
    sds = lambda width, dt: jax.ShapeDtypeStruct((b, t, width), dt)
    return pl.pallas_call(
        _proj_body,
        grid=(b, t // tm),
        in_specs=[tok(d), per_batch, per_batch, row(d), full(w_main), full(w_ba), row(w), row(w), full(e64),
                  row(128), row(128)],
        out_specs=[tok(w), pl.BlockSpec((1, tm // 128, w, 128), lambda bi, i: (bi, i, 0, 0)), tok(w), tok(w),
                   tok(3 * w), tok(w), tok(128), tok(2 * w), tok(2 * w)],
        out_shape=[sds(w, BF16), jax.ShapeDtypeStruct((b, t // 128, w, 128), BF16), sds(w, BF16), sds(w, BF16),
                   sds(3 * w, F32), sds(w, BF16), sds(128, F32), sds(2 * w, BF16), sds(2 * w, BF16)],
        compiler_params=pltpu.CompilerParams(dimension_semantics=("parallel", "parallel"),
                                             vmem_limit_bytes=VMEM_LIMIT),
        name="proj",
    )(x, scale, shift, nw, w_main, w_ba, qw, kw, e64, alog, dtb)


def _bias_body(rpb_ref, o_ref):
    h = pl.program_id(0)
    n_dy, n_dx = 2 * NA_KH - 1, 2 * NA_KW - 1
    cq = lax.broadcasted_iota(jnp.int32, (GRID_W, GRID_W), 0)
    ck = lax.broadcasted_iota(jnp.int32, (GRID_W, GRID_W), 1)
    c0 = jnp.clip(cq - NA_KW // 2, 0, GRID_W - NA_KW)
    col_in = (ck >= c0) & (ck < c0 + NA_KW)
    dx = jnp.clip(ck - cq, -(NA_KW - 1), NA_KW - 1) + (NA_KW - 1)
    for dy in range(n_dy):
        acc = jnp.zeros((GRID_W, GRID_W), F32)
        for d in range(n_dx):
            acc = jnp.where(dx == d, rpb_ref[(h * n_dy + dy) * n_dx + d], acc)
        o_ref[0, dy] = jnp.where(col_in, acc, NEG_INF)


def _bias_tiles(rpb):
    n_dy = 2 * NA_KH - 1
    return pl.pallas_call(
        _bias_body,
        grid=(HEADS,),
        in_specs=[pl.BlockSpec(memory_space=pltpu.SMEM)],
        out_specs=pl.BlockSpec((1, n_dy, GRID_W, GRID_W), lambda h: (h, 0, 0, 0)),
        out_shape=jax.ShapeDtypeStruct((HEADS, n_dy, GRID_W, GRID_W), F32),
        compiler_params=pltpu.CompilerParams(dimension_semantics=("parallel",)),
        name="bias",
    )(rpb.reshape(-1))


NA_BAND = NA_KH + 2
NA_PAIRS_PER_STEP = 4


def _bias_variants(tiles, rows):
    n_var = NA_KH // 2 + 1
    y = np.arange(NA_BAND)
    dy = np.zeros((n_var, 2, NA_BAND), np.int32)
    valid = np.zeros((n_var, 2, NA_BAND), bool)
    for var, m in enumerate((0, 1, 2, rows // 2 - 2, rows // 2 - 1)):
        b0 = min(max(2 * m - NA_KH // 2, 0), rows - NA_BAND)
        assert (2 * m - b0) // 2 == var
        for e in range(2):
            r = 2 * m + e
            r0 = min(max(r - NA_KH // 2, 0), rows - NA_KH)
            valid[var, e] = (b0 + y >= r0) & (b0 + y < r0 + NA_KH)
            dy[var, e] = np.clip(b0 + y - r + NA_KH - 1, 0, 2 * NA_KH - 2)
    g = jnp.where(valid[None, :, :, :, None, None], tiles[:, dy], NEG_INF)
    g = g.reshape((HEADS // 2, 2) + g.shape[1:])
    g = jnp.transpose(g, (0, 2, 3, 1, 5, 4, 6))
    return g.reshape(HEADS // 2, n_var, 4 * GRID_W, NA_BAND * GRID_W)


def _na_body(q_ref, kt_ref, v_ref, kct_ref, vc_ref, bias_ref, o_ref, *, rows):
    j = pl.program_id(2)
    lane = lax.broadcasted_iota(jnp.int32, (GRID_W, 2 * HEAD_DIM), 1)
    first = lane < HEAD_DIM
    kct = jnp.concatenate([kct_ref[0, i] for i in range(kct_ref.shape[1])], axis=1)
    vc = vc_ref[0]
    tiles_per_band = NA_BAND * GRID_W // 128
    pairs = range(NA_PAIRS_PER_STEP)

    lhs, s_win, s_ctx, start = {}, {}, {}, {}
    for mm in pairs:
        m = j * NA_PAIRS_PER_STEP + mm
        b0 = jnp.clip(2 * m - NA_KH // 2, 0, rows - NA_BAND)
        variant = (2 * m - b0) // 2
        start[mm] = b0 // 2
        parts = []
        for e in range(2):
            qr = q_ref[0, (2 * mm + e) * GRID_W:(2 * mm + e + 1) * GRID_W, :]
            zero = jnp.zeros_like(qr)
            parts += [jnp.where(first, qr, zero), jnp.where(first, zero, qr)]
        lhs[mm] = jnp.concatenate(parts, axis=0)
        kt = jnp.concatenate([kt_ref[0, start[mm] + i] for i in range(tiles_per_band)], axis=1)
        s_win[mm] = _dot(lhs[mm], kt) + bias_ref[0, variant]
        s_ctx[mm] = _dot(lhs[mm], kct)
    p_win, p_ctx, denom = {}, {}, {}
    for mm in pairs:
        peak = jnp.maximum(jnp.max(s_win[mm], axis=-1, keepdims=True), jnp.max(s_ctx[mm], axis=-1, keepdims=True))
        p_win[mm] = jnp.exp(s_win[mm] - peak)
        p_ctx[mm] = jnp.exp(s_ctx[mm] - peak)
        denom[mm] = jnp.sum(p_win[mm], axis=-1, keepdims=True) + jnp.sum(p_ctx[mm], axis=-1, keepdims=True)
    for mm in pairs:
        vb = v_ref[0, pl.ds(pl.multiple_of(start[mm] * 128, 128), NA_BAND * GRID_W), :]
        o = (_dot(p_win[mm].astype(BF16), vb) + _dot(p_ctx[mm].astype(BF16), vc)) / denom[mm]
        for e in range(2):
            top = o[2 * e * GRID_W:(2 * e + 1) * GRID_W]
            bottom = o[(2 * e + 1) * GRID_W:(2 * e + 2) * GRID_W]
            o_ref[0, (2 * mm + e) * GRID_W:(2 * mm + e + 1) * GRID_W, :] = jnp.where(first, top, bottom).astype(
                o_ref.dtype)


def _neighbourhood_attention(q, kt, v, kct, vc, bias):
    b, t, _ = q.shape
    ctx_len = vc.shape[1]
    rows = t // GRID_W
    rows_per_step = 2 * NA_PAIRS_PER_STEP
    assert rows >= NA_BAND + 2 and rows % rows_per_step == 0
    tq = rows_per_step * GRID_W
    pair_w = 2 * HEAD_DIM
    return pl.pallas_call(
        functools.partial(_na_body, rows=rows),
        grid=(b, HEADS // 2, t // tq),
        in_specs=[pl.BlockSpec((1, tq, pair_w), lambda bi, p, j: (bi, j, p)),
                  pl.BlockSpec((1, t // 128, pair_w, 128), lambda bi, p, j: (bi, 0, p, 0)),
                  pl.BlockSpec((1, t, pair_w), lambda bi, p, j: (bi, 0, p)),
                  pl.BlockSpec((1, ctx_len // 128, pair_w, 128), lambda bi, p, j: (bi, 0, p, 0)),
                  pl.BlockSpec((1, ctx_len, pair_w), lambda bi, p, j: (bi, 0, p)),
                  pl.BlockSpec((1,) + bias.shape[1:], lambda bi, p, j: (p, 0, 0, 0))],
        out_specs=pl.BlockSpec((1, tq, pair_w), lambda bi, p, j: (bi, j, p)),
        out_shape=jax.ShapeDtypeStruct((b, t, WIDTH), BF16),
        compiler_params=pltpu.CompilerParams(dimension_semantics=("parallel", "parallel", "arbitrary"),
                                             vmem_limit_bytes=VMEM_LIMIT),
        name="na",
    )(q, kt, v, kct, vc, bias)


def _split3(x):
    hi = x.astype(BF16)
    rest = x - hi.astype(F32)
    mid = rest.astype(BF16)
    return hi, mid, (rest - mid.astype(F32)).astype(BF16)


def _prep_body(*refs, rope):
    if rope:
        x_ref, hp_ref, hn_ref, cw_ref, cos_ref, sina_ref, sinb_ref = refs[:7]
    else:
        x_ref, hp_ref, hn_ref, cw_ref = refs[:4]
    e_ref, bg_ref, tril_ref, triu_ref, ex_ref, q_ref, k_ref, v_ref, bf_ref, bb_ref, gf_ref, gb_ref = refs[-12:]
    i = pl.program_id(0)
    last = pl.num_programs(0) - 1
    x = x_ref[0]
    tc = x.shape[0]
    rowid = lax.broadcasted_iota(jnp.int32, x.shape, 0)
    before = jnp.where(i > 0, hp_ref[0, 0, 7:8, :], 0.0)
    after = jnp.where(i < last, hn_ref[0, 0, 0:1, :], 0.0)
    prev = jnp.where(rowid == 0, before, pltpu.roll(x, 1, axis=0))
    nxt = jnp.where(rowid == tc - 1, after, pltpu.roll(x, tc - 1, axis=0))
    y = _silu(prev * cw_ref[0:1, :] + x * cw_ref[1:2, :] + nxt * cw_ref[2:3, :])

    w = WIDTH
    half = HEAD_DIM // 4

    def norm_rope(a):
        a = a * lax.rsqrt(_seg_sum(a * a, e_ref) + EPS)
        if not rope:
            return a
        return (a * cos_ref[...] + pltpu.roll(a, w - half, axis=1) * sina_ref[...]
                + pltpu.roll(a, half, axis=1) * sinb_ref[...])

    q_ref[0] = norm_rope(y[:, :w]) * HEAD_DIM ** -0.5
    k_ref[0] = norm_rope(y[:, w:2 * w])
    v_ref[0] = y[:, 2 * w:]

    bg = bg_ref[0]
    n = bg.shape[1]
    lane = lax.broadcasted_iota(jnp.int32, bg.shape, 1)
    parts = jnp.concatenate(_split3(bg), axis=1)
    cum_f = _dot(tril_ref[...], parts)
    cum_b = _dot(triu_ref[...], parts)
    cum_f = cum_f[:, :n] + cum_f[:, n:2 * n] + cum_f[:, 2 * n:]
    cum_b = cum_b[:, :n] + cum_b[:, n:2 * n] + cum_b[:, 2 * n:]
    src = jnp.where(lane < 2 * HEADS, bg, jnp.where(lane < 3 * HEADS, cum_f, cum_b))
    wide = _dot(jnp.concatenate(_split3(src), axis=0), ex_ref[...])
    wide = wide[:tc] + wide[tc:2 * tc] + wide[2 * tc:]
    bf_ref[0] = wide[:, :w]
    bb_ref[0] = wide[:, w:2 * w]
    gf_ref[0] = wide[:, 2 * w:3 * w]
    gb_ref[0] = wide[:, 3 * w:]


def _prepare(raw, bg, conv_w, tables, e64, tril, triu, expand, tc):
    b, t, cw = raw.shape
    w = WIDTH
    nblk8 = t // 8
    raw4 = raw.reshape(b, nblk8, 8, cw)
    per8 = tc // 8
    tok = lambda width: pl.BlockSpec((1, tc, width), lambda i, bi: (bi, i, 0))
    tab = pl.BlockSpec((tc, w), lambda i, bi: (i, 0))
    full = lambda a: pl.BlockSpec(a.shape, lambda i, bi: (0,) * a.ndim)
    out = jax.ShapeDtypeStruct((b, t, w), F32)
    tables = tuple(tables) if tables is not None else ()
    return pl.pallas_call(
        functools.partial(_prep_body, rope=bool(tables)),
        grid=(t // tc, b),
        in_specs=[tok(cw),
                  pl.BlockSpec((1, 1, 8, cw), lambda i, bi: (bi, jnp.maximum(i * per8 - 1, 0), 0, 0)),
                  pl.BlockSpec((1, 1, 8, cw), lambda i, bi: (bi, jnp.minimum((i + 1) * per8, nblk8 - 1), 0, 0)),
                  full(conv_w)] + [tab] * len(tables) + [full(e64), tok(128), full(tril), full(triu), full(expand)],
        out_specs=[tok(w)] * 7,
        out_shape=[out] * 7,
        compiler_params=pltpu.CompilerParams(dimension_semantics=("parallel", "parallel"),
                                             vmem_limit_bytes=VMEM_LIMIT),
        name="prep",
    )(raw, raw4, raw4, conv_w, *tables, e64, bg, tril, triu, expand)


def _block_diag(x, mask):
    return jnp.concatenate([x.astype(BF16)] * GROUP, axis=0) * mask


def _bdot(a, b):
    return _dot(a.astype(BF16), b.astype(BF16))


def _chunk_body(q_ref, k_ref, v_ref, bf_ref, bb_ref, gf_ref, gb_ref, dmat_ref, mask_ref, *out_refs, chunks):
    dmat = dmat_ref[...]
    bdmask = mask_ref[...]
    eye = (dmat == 0).astype(F32)
    n_out = len(out_refs) // 2
    bd = lambda x: _block_diag(x, bdmask)
    tile = lambda ref, c, grp: ref[0, c * CHUNK:(c + 1) * CHUNK, grp * GW:(grp + 1) * GW]
    tiles = [(c, grp) for c in range(chunks) for grp in range(HEADS // GROUP)]
    units = [(c, grp, d) for c, grp in tiles for d in range(2)]

    both = {t: _dot_nt(jnp.concatenate([tile(k_ref, *t), tile(q_ref, *t)], axis=0).astype(BF16), bd(tile(k_ref, *t)))
            for t in tiles}

    row = lax.broadcasted_iota(jnp.int32, dmat.shape, 0)
    col = row - dmat
    same16, same32 = (row // 16) == (col // 16), (row // 32) == (col // 32)
    qk, inv, diag, off32, off64 = {}, {}, {}, {}, {}
    for c, grp, d in units:
        beta, gc = tile((bf_ref, bb_ref)[d], c, grp), tile((gf_ref, gb_ref)[d], c, grp)
        incl, strict = (dmat <= 0, dmat < 0) if d else (dmat >= 0, dmat > 0)
        g_row = jnp.sum(gc * eye, axis=0, keepdims=True)
        decay = jnp.exp(jnp.where(incl, gc - g_row, NEG_INF))
        kk_qk = both[c, grp]
        low = jnp.where(strict, kk_qk[:CHUNK] * beta * decay, 0.0)
        qk[c, grp, d] = kk_qk[CHUNK:] * decay
        diag[c, grp, d] = jnp.where(same16, low, 0.0)
        off32[c, grp, d] = jnp.where(same32 & ~same16, low, 0.0)
        off64[c, grp, d] = jnp.where(same32, 0.0, low)
        inv[c, grp, d] = eye - diag[c, grp, d]

    power = {un: _bdot(diag[un], bd(diag[un])) for un in units}
    for _ in range(2):
        res = {un: _bdot(jnp.concatenate([power[un], inv[un]], axis=0), bd(power[un])) for un in units}
        power = {un: res[un][:CHUNK] for un in units}
        inv = {un: inv[un] + res[un][CHUNK:] for un in units}
    inv = {un: inv[un] + _bdot(inv[un], bd(power[un])) for un in units}
    for off in (off32, off64):
        half = {un: _bdot(inv[un], bd(off[un])) for un in units}
        inv = {un: inv[un] - _bdot(half[un], bd(inv[un])) for un in units}

    for c, grp, d in units:
        un = (c, grp, d)
        q, k, v = tile(q_ref, c, grp), tile(k_ref, c, grp), tile(v_ref, c, grp)
        beta, gc = tile((bf_ref, bb_ref)[d], c, grp), tile((gf_ref, gb_ref)[d], c, grp)
        last = 0 if d else CHUNK - 1
        eg = jnp.exp(gc)
        g_last = gc[last:last + 1, :]
        u = _bdot(inv[un], bd(v * beta))
        wk = _bdot(inv[un], bd(k * beta * eg))
        k_dec = k * jnp.exp(g_last - gc)
        u_ref, wk_ref, qk_ref, qd_ref, kdt_ref, gt_ref = out_refs[d * n_out:(d + 1) * n_out]
        rows, sl = slice(c * CHUNK, (c + 1) * CHUNK), slice(grp * GW, (grp + 1) * GW)
        u_ref[0, rows, sl] = u.astype(BF16)
        wk_ref[0, rows, sl] = wk.astype(BF16)
        qk_ref[0, rows, sl] = qk[un].astype(BF16)
        qd_ref[0, rows, sl] = (q * eg).astype(BF16)
        kdt_ref[0, c * WIDTH + grp * GW:c * WIDTH + (grp + 1) * GW, :] = k_dec.T.astype(BF16)
        gt_ref[0, c, :, sl] = jnp.exp(g_last)


def _chunk_terms(q, k, v, beta_f, beta_b, gc_f, gc_b, dmat, bdmask, chunks):
    b, t, w = q.shape
    n = t // CHUNK
    tc = chunks * CHUNK
    tok = pl.BlockSpec((1, tc, w), lambda bi, i: (bi, i, 0))
    full = lambda a: pl.BlockSpec(a.shape, lambda bi, i: (0, 0))
    per_dir_specs = [tok] * 4 + [pl.BlockSpec((1, chunks * w, CHUNK), lambda bi, i: (bi, i, 0)),
                                 pl.BlockSpec((1, chunks, 1, w), lambda bi, i: (bi, i, 0, 0))]
    act = jax.ShapeDtypeStruct((b, t, w), BF16)
    per_dir_shapes = [act] * 4 + [jax.ShapeDtypeStruct((b, n * w, CHUNK), BF16),
                                  jax.ShapeDtypeStruct((b, n, 1, w), F32)]
    return pl.pallas_call(
        functools.partial(_chunk_body, chunks=chunks),
        grid=(b, t // tc),
        in_specs=[tok] * 7 + [full(dmat), full(bdmask)],
        out_specs=per_dir_specs * 2,
        out_shape=per_dir_shapes * 2,
        compiler_params=pltpu.CompilerParams(dimension_semantics=("parallel", "parallel"),
                                             vmem_limit_bytes=VMEM_LIMIT),
        name="chunk",
    )(q, k, v, beta_f, beta_b, gc_f, gc_b, dmat, bdmask)


def _seq_body(*refs, batch):
    fwd, bwd = refs[0:6], refs[6:12]
    s0_ref, mask_ref, of_ref, ob_ref, s_ref = refs[12:]

    @pl.when(pl.program_id(0) == 0)
    def _():
        s_ref[...] = s0_ref[...]

    bdmask = mask_ref[...]
    keep = bdmask.astype(F32)
    dirs = ((fwd, of_ref), (bwd, ob_ref))
    units = [(b, d, grp) for b in range(batch) for d in range(2) for grp in range(HEADS // GROUP)]
    lanes = lambda grp: slice(grp * GW, (grp + 1) * GW)

    proj = {}
    for b, d, grp in units:
        _, wk_ref, _, qd_ref, _, _ = dirs[d][0]
        lhs = jnp.concatenate([wk_ref[b, :, lanes(grp)], qd_ref[b, :, lanes(grp)]], axis=0)
        proj[b, d, grp] = _bdot(lhs, s_ref[b, d, grp])
    v_new = {}
    for b, d, grp in units:
        v_new[b, d, grp] = dirs[d][0][0][b, :, lanes(grp)].astype(F32) - proj[b, d, grp][:CHUNK]
    for b, d, grp in units:
        (_, _, qk_ref, _, kdt_ref, gt_ref), o_ref = dirs[d]
        vn = v_new[b, d, grp]
        update = _bdot(kdt_ref[b, lanes(grp), :], vn)
        s_ref[b, d, grp] = s_ref[b, d, grp] * gt_ref[b, 0, :, lanes(grp)] + update * keep
        o_ref[b, :, lanes(grp)] = proj[b, d, grp][CHUNK:] + _bdot(qk_ref[b, :, lanes(grp)], _block_diag(vn, bdmask))


def _delta_scan(terms, s0, bdmask):
    b, t, w = terms[0].shape
    n = t // CHUNK

    def specs(idx):
        return ([pl.BlockSpec((b, CHUNK, w), lambda i: (0, idx(i), 0))] * 4
                + [pl.BlockSpec((b, w, CHUNK), lambda i: (0, idx(i), 0)),
                   pl.BlockSpec((b, 1, 1, w), lambda i: (0, idx(i), 0, 0))])

    forward, backward = (lambda i: i), (lambda i: n - 1 - i)
    st = pl.BlockSpec(s0.shape, lambda i: (0,) * s0.ndim)
    o = jax.ShapeDtypeStruct((b, t, w), F32)
    return pl.pallas_call(
        functools.partial(_seq_body, batch=b),
        grid=(n,),
        in_specs=specs(forward) + specs(backward) + [st, pl.BlockSpec(bdmask.shape, lambda i: (0, 0))],
        out_specs=[specs(forward)[0], specs(backward)[0], st],
        out_shape=[o, o, jax.ShapeDtypeStruct(s0.shape, F32)],
        compiler_params=pltpu.CompilerParams(dimension_semantics=("arbitrary",), vmem_limit_bytes=VMEM_LIMIT),
        name="seq",
    )(*terms, s0, bdmask)


def _merge_body(x_ref, ona_ref, gzna_ref, of_ref, ob_ref, gzdn_ref, sgna_ref, sgdn_ref, gate_ref, dnw_ref, e_ref,
                wna_ref, wdn_ref, wout_ref, o_ref):
    a = (ona_ref[0].astype(F32) * gzna_ref[0].astype(F32)).astype(BF16)
    u_na = _dot(a, wna_ref[...])
    od = of_ref[0] + ob_ref[0]
    odn = od * lax.rsqrt(_seg_sum(od * od, e_ref) * (1.0 / HEAD_DIM) + EPS) * dnw_ref[...]
    u_dn = _dot((odn * gzdn_ref[0].astype(F32)).astype(BF16), wdn_ref[...])
    y = sgna_ref[0].astype(F32) * u_na + sgdn_ref[0].astype(F32) * u_dn
    o_ref[0] = x_ref[0] + gate_ref[0] * _dot(y.astype(BF16), wout_ref[...])


def _merge(x, o_na, gz_na, o_f, o_b, gz_dn, sg_na, sg_dn, gate, dnw, e64, w_o_na, w_o_dn, w_out, tm):
    b, t, d = x.shape
    w = WIDTH
    tok = lambda width: pl.BlockSpec((1, tm, width), lambda bi, i: (bi, i, 0))
    full = lambda a: pl.BlockSpec(a.shape, lambda bi, i: (0, 0))
    return pl.pallas_call(
        _merge_body,
        grid=(b, t // tm),
        in_specs=[tok(d), tok(w), tok(w), tok(w), tok(w), tok(w), tok(d), tok(d),
                  pl.BlockSpec((1, 1, d), lambda bi, i: (bi, 0, 0)), full(dnw), full(e64),
                  full(w_o_na), full(w_o_dn), full(w_out)],
        out_specs=tok(d),
        out_shape=jax.ShapeDtypeStruct((b, t, d), F32),
        compiler_params=pltpu.CompilerParams(dimension_semantics=("parallel", "parallel"),
                                             vmem_limit_bytes=VMEM_LIMIT),
        name="merge",
    )(x, o_na, gz_na, o_f, o_b, gz_dn, sg_na, sg_dn, gate, dnw, e64, w_o_na, w_o_dn, w_out)


def _constants(tc):
    seg = np.arange(WIDTH) // HEAD_DIM
    e64 = (seg[:, None] == seg[None, :]).astype(np.float32)
    tok = np.arange(tc)
    same_chunk = (tok[:, None] // CHUNK) == (tok[None, :] // CHUNK)
    tril = (same_chunk & (tok[None, :] <= tok[:, None])).astype(np.float32)
    triu = (same_chunk & (tok[None, :] >= tok[:, None])).astype(np.float32)
    expand = np.zeros((128, 4 * WIDTH), np.float32)
    for s in range(4):
        for h in range(HEADS):
            expand[s * HEADS + h, s * WIDTH + h * HEAD_DIM:s * WIDTH + (h + 1) * HEAD_DIM] = 1.0
    lane = np.arange(GW)
    dmat = (np.arange(CHUNK)[:, None] - (lane % HEAD_DIM)[None, :]).astype(np.int32)
    bdmask = ((lane[:, None] // HEAD_DIM) == (lane[None, :] // HEAD_DIM)).astype(np.float32)
    return (jnp.asarray(e64, BF16), jnp.asarray(tril, BF16), jnp.asarray(triu, BF16), jnp.asarray(expand, BF16),
            jnp.asarray(dmat), jnp.asarray(bdmask, BF16))


def _rope_tables(t):
    half = HEAD_DIM // 4
    tok = jnp.arange(t)
    freqs = ROPE_BASE ** (-jnp.arange(half, dtype=F32) / half)
    ang_r = (tok // GRID_W).astype(F32)[:, None] * freqs[None, :]
    ang_c = (tok % GRID_W).astype(F32)[:, None] * freqs[None, :]
    cos = jnp.concatenate([jnp.cos(ang_r)] * 2 + [jnp.cos(ang_c)] * 2, axis=-1)
    sin = jnp.concatenate([jnp.sin(ang_r)] * 2 + [jnp.sin(ang_c)] * 2, axis=-1)
    lower = (jnp.arange(HEAD_DIM) % (2 * half)) < half
    sina = jnp.where(lower, -sin, 0.0)
    sinb = jnp.where(lower, 0.0, sin)
    tile = lambda a: jnp.tile(a, (1, HEADS))
    return tile(cos), tile(sina), tile(sinb)


def _pad_lanes(a, offset):
    return jnp.zeros((1, 128), F32).at[0, offset:offset + a.size].set(a.reshape(-1))


def _layer(x, ctx, c, c_ctx, mod_w, mod_b, norm_w, w_in, conv_w, na_q_norm, na_k_norm, na_rpb, dn_A_log, dn_dt_bias,
           dn_norm_w, w_o_na, w_o_dn, w_out):
    b, t, d = x.shape
    w = WIDTH
    tm = 256
    tc = 256
    e64, tril, triu, expand, dmat, bdmask = _constants(tc)

    cc = jnp.zeros((8, d), F32).at[:b].set(c).at[b].set(c_ctx)
    mod = _modulation(cc, mod_w, mod_b)
    shift, scale, gate = mod[:, :d], mod[:, d:2 * d], mod[:, 2 * d:]
    rows_x = lambda a: a[:b, None, :]
    rows_c = lambda a: jnp.broadcast_to(a[b][None, None, :], (b, 1, d))

    n_ba = 4 * HEADS
    ba0 = 8 * w
    w_main = jnp.concatenate([w_in[:, :ba0], w_in[:, ba0 + n_ba:]], axis=1).astype(BF16)
    w_ba = jnp.zeros((d, 128), F32).at[:, :n_ba].set(w_in[:, ba0:ba0 + n_ba]).astype(BF16)
    nw = norm_w.reshape(1, d)
    qw = jnp.tile(na_q_norm, HEADS).reshape(1, w)
    kw = jnp.tile(na_k_norm, HEADS).reshape(1, w)
    alog = _pad_lanes(dn_A_log, 2 * HEADS)
    dtb = _pad_lanes(dn_dt_bias, 2 * HEADS)
    project = functools.partial(_project, nw=nw, w_main=w_main, w_ba=w_ba, qw=qw, kw=kw, e64=e64, alog=alog,
                                dtb=dtb, tm=tm)
    q_na, k_na, v_na, gz_na, dn_raw, gz_dn, bg, sg_na, sg_dn = project(x, rows_x(scale), rows_x(shift))
    _, k_c, v_c, _, dn_raw_c, _, bg_c, _, _ = project(ctx, rows_c(scale), rows_c(shift))

    bias = _bias_variants(_bias_tiles(na_rpb), t // GRID_W)
    o_na = _neighbourhood_attention(q_na, k_na, v_na, k_c, v_c, bias)

    prepare = functools.partial(_prepare, conv_w=conv_w, e64=e64, tril=tril, triu=triu, expand=expand, tc=tc)
    pc = prepare(dn_raw_c, bg_c, tables=None)
    px = prepare(dn_raw, bg, tables=_rope_tables(t))
    s_zero = jnp.zeros((b, 2, HEADS // GROUP, GW, GW), F32)
    _, _, s_ctx = _delta_scan(_chunk_terms(*pc, dmat, bdmask, chunks=2), s_zero, bdmask)
    o_f, o_b, _ = _delta_scan(_chunk_terms(*px, dmat, bdmask, chunks=2), s_ctx, bdmask)

    dnw = jnp.tile(dn_norm_w, HEADS).reshape(1, w)
    return _merge(x, o_na, gz_na, o_f, o_b, gz_dn, sg_na, sg_dn, rows_x(gate), dnw, e64,
                  w_o_na.astype(BF16), w_o_dn.astype(BF16), w_out.astype(BF16), tm)


def kernel(x, c, ctx, c_ctx, mod_w, mod_b, norm_w, w_in, conv_w, na_q_norm, na_k_norm, na_rpb, dn_A_log, dn_dt_bias,
           dn_norm_w, w_o_na, w_o_dn, w_out):
    depth = mod_w.shape[0]
    assert depth == 1, "context-stream update between layers is not implemented"
    return _layer(x, ctx, c, c_ctx, mod_w[0], mod_b[0], norm_w[0], w_in[0], conv_w[0], na_q_norm[0], na_k_norm[0],
                  na_rpb[0], dn_A_log[0], dn_dt_bias[0], dn_norm_w[0], w_o_na[0], w_o_dn[0], w_out[0])
```

```python
import functools

import jax
import jax.numpy as jnp
import numpy as np
from jax import lax
from jax.experimental import pallas as pl
from jax.experimental.pallas import tpu as pltpu

F32 = jnp.float32
BF16 = jnp.bfloat16
EPS = 1e-6
GRID_W = 64
HEADS = 8
HEAD_DIM = 64
WIDTH = HEADS * HEAD_DIM
NA_KH = 8
NA_KW = 16
CHUNK = 64
CONV_K = 3
ROPE_BASE = 10000.0
GROUP = 4
GW = GROUP * HEAD_DIM
NEG_INF = float("-inf")
HIGHEST = lax.Precision.HIGHEST
VMEM_LIMIT = 56 * 1024 * 1024


def _dot(a, b, **kw):
    return jnp.dot(a, b, preferred_element_type=F32, **kw)


def _dot_nt(a, b):
    return lax.dot_general(a, b, (((1,), (1,)), ((), ())), preferred_element_type=F32)


def _dot_tn(a, b):
    return lax.dot_general(a, b, (((0,), (0,)), ((), ())), preferred_element_type=F32)


def _seg_sum(x, e_ref):
    return _dot(x.astype(BF16), e_ref[...])


def _silu(x):
    return x * jax.nn.sigmoid(x)


def _mod_body(c_ref, w_ref, b_ref, o_ref):
    o_ref[...] = _dot(_silu(c_ref[...]), w_ref[...], precision=HIGHEST) + b_ref[...]


def _modulation(cc, mod_w, mod_b):
    rows, d = cc.shape
    n = mod_w.shape[1]
    tn = 512
    return pl.pallas_call(
        _mod_body,
        grid=(n // tn,),
        in_specs=[pl.BlockSpec((rows, d), lambda j: (0, 0)),
                  pl.BlockSpec((d, tn), lambda j: (0, j)),
                  pl.BlockSpec((1, tn), lambda j: (0, j))],
        out_specs=pl.BlockSpec((rows, tn), lambda j: (0, j)),
        out_shape=jax.ShapeDtypeStruct((rows, n), F32),
        compiler_params=pltpu.CompilerParams(dimension_semantics=("parallel",)),
        name="mod",
    )(cc, mod_w, mod_b.reshape(1, n))


def _proj_body(x_ref, scale_ref, shift_ref, nw_ref, w_ref, wba_ref, qw_ref, kw_ref, e_ref, alog_ref, dtb_ref,
               q_ref, kt_ref, v_ref, gzna_ref, dn_ref, gzdn_ref, bg_ref, sgna_ref, sgdn_ref):
    x = x_ref[0]
    xn = x * lax.rsqrt(jnp.mean(x * x, axis=-1, keepdims=True) + EPS)
    h = (xn * nw_ref[...]) * (1.0 + scale_ref[0]) + shift_ref[0]
    hb = h.astype(BF16)

    def mm(lo, hi):
        return _dot(hb, w_ref[:, lo:hi])

    def head_rms(a, w_row):
        return a * lax.rsqrt(_seg_sum(a * a, e_ref) * (1.0 / HEAD_DIM) + EPS) * w_row

    w = WIDTH
    q_ref[0] = (head_rms(mm(0, w), qw_ref[...]) * HEAD_DIM ** -0.5).astype(BF16)
    k_t = head_rms(mm(w, 2 * w), kw_ref[...]).T.astype(BF16)
    for i in range(kt_ref.shape[1]):
        kt_ref[0, i] = k_t[:, i * 128:(i + 1) * 128]
    v_ref[0] = mm(2 * w, 3 * w).astype(BF16)
    gzna_ref[0] = _silu(mm(3 * w, 4 * w)).astype(BF16)
    dn_ref[0] = mm(4 * w, 7 * w)
    gzdn_ref[0] = _silu(mm(7 * w, 8 * w)).astype(BF16)
    sgna_ref[0] = jax.nn.sigmoid(mm(8 * w, 10 * w)).astype(BF16)
    sgdn_ref[0] = jax.nn.sigmoid(mm(10 * w, 12 * w)).astype(BF16)

    ba = _dot(hb, wba_ref[...])
    lane = lax.broadcasted_iota(jnp.int32, ba.shape, 1)
    a = ba + dtb_ref[...]
    softplus = jnp.maximum(a, 0.0) + jnp.log1p(jnp.exp(-jnp.abs(a)))
    g = -jnp.exp(alog_ref[...]) * softplus
    bg_ref[0] = jnp.where(lane < 2 * HEADS, jax.nn.sigmoid(ba), jnp.where(lane < 4 * HEADS, g, 0.0))


def _project(x, scale, shift, nw, w_main, w_ba, qw, kw, e64, alog, dtb, tm):
    b, t, d = x.shape
    w = WIDTH
    tok = lambda width: pl.BlockSpec((1, tm, width), lambda bi, i: (bi, i, 0))
    row = lambda width: pl.BlockSpec((1, width), lambda bi, i: (0, 0))
    per_batch = pl.BlockSpec((1, 1, d), lambda bi, i: (bi, 0, 0))
    full = lambda a: pl.BlockSpec(a.shape, lambda bi, i: (0, 0))
    sds = lambda width, dt: jax.ShapeDtypeStruct((b, t, width), dt)
    return pl.pallas_call(
        _proj_body,
        grid=(b, t // tm),
        in_specs=[tok(d), per_batch, per_batch, row(d), full(w_main), full(w_ba), row(w), row(w), full(e64),
                  row(128), row(128)],
        out_specs=[tok(w), pl.BlockSpec((1, tm // 128, w, 128), lambda bi, i: (bi, i, 0, 0)), tok(w), tok(w),
                   tok(3 * w), tok(w), tok(128), tok(2 * w), tok(2 * w)],
        out_shape=[sds(w, BF16), jax.ShapeDtypeStruct((b, t // 128, w, 128), BF16), sds(w, BF16), sds(w, BF16),
                   sds(3 * w, F32), sds(w, BF16), sds(128, F32), sds(2 * w, BF16), sds(2 * w, BF16)],
        compiler_params=pltpu.CompilerParams(dimension_semantics=("parallel", "parallel"),
                                             vmem_limit_bytes=VMEM_LIMIT),
        name="proj",
    )(x, scale, shift, nw, w_main, w_ba, qw, kw, e64, alog, dtb)


NA_BAND = NA_KH + 2
NA_PAIRS_PER_STEP = 4
NA_BAND_OFFSETS = NA_KH // 2 + 1


def _band_plan(rows):
    y = np.arange(NA_BAND)
    dy = np.zeros((NA_BAND_OFFSETS, 2, NA_BAND), np.int32)
    valid = np.zeros((NA_BAND_OFFSETS, 2, NA_BAND), bool)
    for var, m in enumerate((0, 1, 2, rows // 2 - 2, rows // 2 - 1)):
        b0 = min(max(2 * m - NA_KH // 2, 0), rows - NA_BAND)
        assert (2 * m - b0) // 2 == var
        for e in range(2):
            r = 2 * m + e
            r0 = min(max(r - NA_KH // 2, 0), rows - NA_KH)
            valid[var, e] = (b0 + y >= r0) & (b0 + y < r0 + NA_KH)
            dy[var, e] = np.clip(b0 + y - r + NA_KH - 1, 0, 2 * NA_KH - 2)
    return dy, valid


def _bias_body(rpb_ref, o_ref, tiles_ref, *, dy, valid):
    p = pl.program_id(0)
    n_dy, n_dx = 2 * NA_KH - 1, 2 * NA_KW - 1
    cq = lax.broadcasted_iota(jnp.int32, (GRID_W, GRID_W), 0)
    ck = lax.broadcasted_iota(jnp.int32, (GRID_W, GRID_W), 1)
    c0 = jnp.clip(cq - NA_KW // 2, 0, GRID_W - NA_KW)
    col_in = (ck >= c0) & (ck < c0 + NA_KW)
    dx = jnp.clip(ck - cq, -(NA_KW - 1), NA_KW - 1) + (NA_KW - 1)
    for hh in range(2):
        for i in range(n_dy):
            acc = jnp.zeros((GRID_W, GRID_W), F32)
            for d in range(n_dx):
                acc = jnp.where(dx == d, rpb_ref[((2 * p + hh) * n_dy + i) * n_dx + d], acc)
            tiles_ref[hh, i] = jnp.where(col_in, acc, NEG_INF)
    outside = jnp.full((GRID_W, GRID_W), NEG_INF, F32)
    for var in range(NA_BAND_OFFSETS):
        for e in range(2):
            for hh in range(2):
                r = (2 * e + hh) * GRID_W
                for y in range(NA_BAND):
                    tile = tiles_ref[hh, int(dy[var, e, y])] if valid[var, e, y] else outside
                    o_ref[0, var, r:r + GRID_W, y * GRID_W:(y + 1) * GRID_W] = tile


def _bias_table(rpb, rows):
    dy, valid = _band_plan(rows)
    shape = (HEADS // 2, NA_BAND_OFFSETS, 4 * GRID_W, NA_BAND * GRID_W)
    return pl.pallas_call(
        functools.partial(_bias_body, dy=dy, valid=valid),
        grid=(HEADS // 2,),
        in_specs=[pl.BlockSpec(memory_space=pltpu.SMEM)],
        out_specs=pl.BlockSpec((1,) + shape[1:], lambda p: (p, 0, 0, 0)),
        out_shape=jax.ShapeDtypeStruct(shape, F32),
        scratch_shapes=[pltpu.VMEM((2, 2 * NA_KH - 1, GRID_W, GRID_W), F32)],
        compiler_params=pltpu.CompilerParams(dimension_semantics=("parallel",)),
        name="bias",
    )(rpb.reshape(-1))


def _na_body(q_ref, kt_ref, v_ref, kct_ref, vc_ref, bias_ref, o_ref, *, rows):
    j = pl.program_id(2)
    lane = lax.broadcasted_iota(jnp.int32, (GRID_W, 2 * HEAD_DIM), 1)
    first = lane < HEAD_DIM
    kct = jnp.concatenate([kct_ref[0, i] for i in range(kct_ref.shape[1])], axis=1)
    vc = vc_ref[0]
    tiles_per_band = NA_BAND * GRID_W // 128
    pairs = range(NA_PAIRS_PER_STEP)

    lhs, s_win, s_ctx, start = {}, {}, {}, {}
    for mm in pairs:
        m = j * NA_PAIRS_PER_STEP + mm
        b0 = jnp.clip(2 * m - NA_KH // 2, 0, rows - NA_BAND)
        variant = (2 * m - b0) // 2
        start[mm] = b0 // 2
        parts = []
        for e in range(2):
            qr = q_ref[0, (2 * mm + e) * GRID_W:(2 * mm + e + 1) * GRID_W, :]
            zero = jnp.zeros_like(qr)
            parts += [jnp.where(first, qr, zero), jnp.where(first, zero, qr)]
        lhs[mm] = jnp.concatenate(parts, axis=0)
        kt = jnp.concatenate([kt_ref[0, start[mm] + i] for i in range(tiles_per_band)], axis=1)
        s_win[mm] = _dot(lhs[mm], kt) + bias_ref[0, variant]
        s_ctx[mm] = _dot(lhs[mm], kct)
    p_win, p_ctx, denom = {}, {}, {}
    for mm in pairs:
        peak = jnp.maximum(jnp.max(s_win[mm], axis=-1, keepdims=True), jnp.max(s_ctx[mm], axis=-1, keepdims=True))
        p_win[mm] = jnp.exp(s_win[mm] - peak)
        p_ctx[mm] = jnp.exp(s_ctx[mm] - peak)
        denom[mm] = jnp.sum(p_win[mm], axis=-1, keepdims=True) + jnp.sum(p_ctx[mm], axis=-1, keepdims=True)
    for mm in pairs:
        vb = v_ref[0, pl.ds(pl.multiple_of(start[mm] * 128, 128), NA_BAND * GRID_W), :]
        o = (_dot(p_win[mm].astype(BF16), vb) + _dot(p_ctx[mm].astype(BF16), vc)) / denom[mm]
        for e in range(2):
            top = o[2 * e * GRID_W:(2 * e + 1) * GRID_W]
            bottom = o[(2 * e + 1) * GRID_W:(2 * e + 2) * GRID_W]
            o_ref[0, (2 * mm + e) * GRID_W:(2 * mm + e + 1) * GRID_W, :] = jnp.where(first, top, bottom).astype(
                o_ref.dtype)


def _neighbourhood_attention(q, kt, v, kct, vc, bias):
    b, t, _ = q.shape
    ctx_len = vc.shape[1]
    rows = t // GRID_W
    rows_per_step = 2 * NA_PAIRS_PER_STEP
    assert rows >= NA_BAND + 2 and rows % rows_per_step == 0
    tq = rows_per_step * GRID_W
    pair_w = 2 * HEAD_DIM
    return pl.pallas_call(
        functools.partial(_na_body, rows=rows),
        grid=(b, HEADS // 2, t // tq),
        in_specs=[pl.BlockSpec((1, tq, pair_w), lambda bi, p, j: (bi, j, p)),
                  pl.BlockSpec((1, t // 128, pair_w, 128), lambda bi, p, j: (bi, 0, p, 0)),
                  pl.BlockSpec((1, t, pair_w), lambda bi, p, j: (bi, 0, p)),
                  pl.BlockSpec((1, ctx_len // 128, pair_w, 128), lambda bi, p, j: (bi, 0, p, 0)),
                  pl.BlockSpec((1, ctx_len, pair_w), lambda bi, p, j: (bi, 0, p)),
                  pl.BlockSpec((1,) + bias.shape[1:], lambda bi, p, j: (p, 0, 0, 0))],
        out_specs=pl.BlockSpec((1, tq, pair_w), lambda bi, p, j: (bi, j, p)),
        out_shape=jax.ShapeDtypeStruct((b, t, WIDTH), BF16),
        compiler_params=pltpu.CompilerParams(dimension_semantics=("parallel", "parallel", "arbitrary"),
                                             vmem_limit_bytes=VMEM_LIMIT),
        name="na",
    )(q, kt, v, kct, vc, bias)


def _split3(x):
    hi = x.astype(BF16)
    rest = x - hi.astype(F32)
    mid = rest.astype(BF16)
    return hi, mid, (rest - mid.astype(F32)).astype(BF16)


def _prep_body(*refs, rope):
    if rope:
        x_ref, hp_ref, hn_ref, cw_ref, cos_ref, sina_ref, sinb_ref = refs[:7]
    else:
        x_ref, hp_ref, hn_ref, cw_ref = refs[:4]
    e_ref, bg_ref, tril_ref, triu_ref, ex_ref, q_ref, k_ref, v_ref, bf_ref, bb_ref, gf_ref, gb_ref = refs[-12:]
    i = pl.program_id(0)
    last = pl.num_programs(0) - 1
    x = x_ref[0]
    tc = x.shape[0]
    rowid = lax.broadcasted_iota(jnp.int32, x.shape, 0)
    before = jnp.where(i > 0, hp_ref[0, 0, 7:8, :], 0.0)
    after = jnp.where(i < last, hn_ref[0, 0, 0:1, :], 0.0)
    prev = jnp.where(rowid == 0, before, pltpu.roll(x, 1, axis=0))
    nxt = jnp.where(rowid == tc - 1, after, pltpu.roll(x, tc - 1, axis=0))
    y = _silu(prev * cw_ref[0:1, :] + x * cw_ref[1:2, :] + nxt * cw_ref[2:3, :])

    w = WIDTH
    half = HEAD_DIM // 4

    def norm_rope(a):
        a = a * lax.rsqrt(_seg_sum(a * a, e_ref) + EPS)
        if not rope:
            return a
        heads = lambda ref: jnp.concatenate([ref[...]] * (w // ref.shape[1]), axis=1)
        return (a * heads(cos_ref) + pltpu.roll(a, w - half, axis=1) * heads(sina_ref)
                + pltpu.roll(a, half, axis=1) * heads(sinb_ref))

    q_ref[0] = norm_rope(y[:, :w]) * HEAD_DIM ** -0.5
    k_ref[0] = norm_rope(y[:, w:2 * w])
    v_ref[0] = y[:, 2 * w:]

    bg = bg_ref[0]
    n = bg.shape[1]
    lane = lax.broadcasted_iota(jnp.int32, bg.shape, 1)
    parts = jnp.concatenate(_split3(bg), axis=1)
    cum_f = _dot(tril_ref[...], parts)
    cum_b = _dot(triu_ref[...], parts)
    cum_f = cum_f[:, :n] + cum_f[:, n:2 * n] + cum_f[:, 2 * n:]
    cum_b = cum_b[:, :n] + cum_b[:, n:2 * n] + cum_b[:, 2 * n:]
    src = jnp.where(lane < 2 * HEADS, bg, jnp.where(lane < 3 * HEADS, cum_f, cum_b))
    hi, mid, _ = _split3(src)
    beta = _dot(hi, ex_ref[:, :2 * w])
    decay = _dot(jnp.concatenate([hi, mid], axis=0), ex_ref[:, 2 * w:])
    decay = decay[:tc] + decay[tc:]
    bf_ref[0] = beta[:, :w]
    bb_ref[0] = beta[:, w:]
    gf_ref[0] = decay[:, :w]
    gb_ref[0] = decay[:, w:]


def _prepare(raw, bg, conv_w, tables, e64, tril, triu, expand, tc):
    b, t, cw = raw.shape
    w = WIDTH
    nblk8 = t // 8
    raw4 = raw.reshape(b, nblk8, 8, cw)
    per8 = tc // 8
    tok = lambda width: pl.BlockSpec((1, tc, width), lambda i, bi: (bi, i, 0))
    tab = pl.BlockSpec((tc, 2 * HEAD_DIM), lambda i, bi: (i, 0))
    full = lambda a: pl.BlockSpec(a.shape, lambda i, bi: (0,) * a.ndim)
    out = jax.ShapeDtypeStruct((b, t, w), F32)
    tables = tuple(tables) if tables is not None else ()
    return pl.pallas_call(
        functools.partial(_prep_body, rope=bool(tables)),
        grid=(t // tc, b),
        in_specs=[tok(cw),
                  pl.BlockSpec((1, 1, 8, cw), lambda i, bi: (bi, jnp.maximum(i * per8 - 1, 0), 0, 0)),
                  pl.BlockSpec((1, 1, 8, cw), lambda i, bi: (bi, jnp.minimum((i + 1) * per8, nblk8 - 1), 0, 0)),
                  full(conv_w)] + [tab] * len(tables) + [full(e64), tok(128), full(tril), full(triu), full(expand)],
        out_specs=[tok(w)] * 7,
        out_shape=[out] * 7,
        compiler_params=pltpu.CompilerParams(dimension_semantics=("parallel", "parallel"),
                                             vmem_limit_bytes=VMEM_LIMIT),
        name="prep",
    )(raw, raw4, raw4, conv_w, *tables, e64, bg, tril, triu, expand)


def _block_diag(x, mask):
    return jnp.concatenate([x.astype(BF16)] * GROUP, axis=0) * mask


def _bdot(a, b):
    return _dot(a.astype(BF16), b.astype(BF16))


def _chunk_body(q_ref, k_ref, v_ref, bf_ref, bb_ref, gf_ref, gb_ref, dmat_ref, mask_ref, *out_refs, chunks):
    dmat = dmat_ref[...]
    bdmask = mask_ref[...]
    eye = (dmat == 0).astype(F32)
    n_out = len(out_refs) // 2
    bd = lambda x: _block_diag(x, bdmask)
    tile = lambda ref, c, grp: ref[0, c * CHUNK:(c + 1) * CHUNK, grp * GW:(grp + 1) * GW]
    tiles = [(c, grp) for c in range(chunks) for grp in range(HEADS // GROUP)]
    units = [(c, grp, d) for c, grp in tiles for d in range(2)]

    both = {t: _dot_nt(jnp.concatenate([tile(k_ref, *t), tile(q_ref, *t)], axis=0).astype(BF16), bd(tile(k_ref, *t)))
            for t in tiles}

    row = lax.broadcasted_iota(jnp.int32, dmat.shape, 0)
    col = row - dmat
    same16, same32 = (row // 16) == (col // 16), (row // 32) == (col // 32)
    qk, inv, diag, off32, off64 = {}, {}, {}, {}, {}
    for c, grp, d in units:
        beta, gc = tile((bf_ref, bb_ref)[d], c, grp), tile((gf_ref, gb_ref)[d], c, grp)
        incl, strict = (dmat <= 0, dmat < 0) if d else (dmat >= 0, dmat > 0)
        g_row = jnp.sum(gc * eye, axis=0, keepdims=True)
        decay = jnp.exp(jnp.where(incl, gc - g_row, NEG_INF))
        kk_qk = both[c, grp]
        low = jnp.where(strict, kk_qk[:CHUNK] * beta * decay, 0.0)
        qk[c, grp, d] = kk_qk[CHUNK:] * decay
        diag[c, grp, d] = jnp.where(same16, low, 0.0)
        off32[c, grp, d] = jnp.where(same32 & ~same16, low, 0.0)
        off64[c, grp, d] = jnp.where(same32, 0.0, low)
        inv[c, grp, d] = eye - diag[c, grp, d]

    power = {un: _bdot(diag[un], bd(diag[un])) for un in units}
    for _ in range(2):
        res = {un: _bdot(jnp.concatenate([power[un], inv[un]], axis=0), bd(power[un])) for un in units}
        power = {un: res[un][:CHUNK] for un in units}
        inv = {un: inv[un] + res[un][CHUNK:] for un in units}
    inv = {un: inv[un] + _bdot(inv[un], bd(power[un])) for un in units}
    for off in (off32, off64):
        half = {un: _bdot(inv[un], bd(off[un])) for un in units}
        inv = {un: inv[un] - _bdot(half[un], bd(inv[un])) for un in units}

    for c, grp, d in units:
        un = (c, grp, d)
        q, k, v = tile(q_ref, c, grp), tile(k_ref, c, grp), tile(v_ref, c, grp)
        beta, gc = tile((bf_ref, bb_ref)[d], c, grp), tile((gf_ref, gb_ref)[d], c, grp)
        last = 0 if d else CHUNK - 1
        eg = jnp.exp(gc)
        g_last = gc[last:last + 1, :]
        u = _bdot(inv[un], bd(v * beta))
        wk = _bdot(inv[un], bd(k * beta * eg))
        k_dec = k * jnp.exp(g_last - gc)
        u_ref, wk_ref, qk_ref, qd_ref, kdt_ref, gt_ref = out_refs[d * n_out:(d + 1) * n_out]
        rows, sl = slice(c * CHUNK, (c + 1) * CHUNK), slice(grp * GW, (grp + 1) * GW)
        u_ref[0, rows, sl] = u.astype(BF16)
        wk_ref[0, rows, sl] = wk.astype(BF16)
        qk_ref[0, rows, sl] = qk[un].astype(BF16)
        qd_ref[0, rows, sl] = (q * eg).astype(BF16)
        kdt_ref[0, c * WIDTH + grp * GW:c * WIDTH + (grp + 1) * GW, :] = k_dec.T.astype(BF16)
        gt_ref[0, c, :, sl] = jnp.exp(g_last)


def _chunk_terms(q, k, v, beta_f, beta_b, gc_f, gc_b, dmat, bdmask, chunks):
    b, t, w = q.shape
    n = t // CHUNK
    tc = chunks * CHUNK
    tok = pl.BlockSpec((1, tc, w), lambda bi, i: (bi, i, 0))
    full = lambda a: pl.BlockSpec(a.shape, lambda bi, i: (0, 0))
    per_dir_specs = [tok] * 4 + [pl.BlockSpec((1, chunks * w, CHUNK), lambda bi, i: (bi, i, 0)),
                                 pl.BlockSpec((1, chunks, 1, w), lambda bi, i: (bi, i, 0, 0))]
    act = jax.ShapeDtypeStruct((b, t, w), BF16)
    per_dir_shapes = [act] * 4 + [jax.ShapeDtypeStruct((b, n * w, CHUNK), BF16),
                                  jax.ShapeDtypeStruct((b, n, 1, w), F32)]
    return pl.pallas_call(
        functools.partial(_chunk_body, chunks=chunks),
        grid=(b, t // tc),
        in_specs=[tok] * 7 + [full(dmat), full(bdmask)],
        out_specs=per_dir_specs * 2,
        out_shape=per_dir_shapes * 2,
        compiler_params=pltpu.CompilerParams(dimension_semantics=("parallel", "parallel"),
                                             vmem_limit_bytes=VMEM_LIMIT),
        name="chunk",
    )(q, k, v, beta_f, beta_b, gc_f, gc_b, dmat, bdmask)


def _seq_body(*refs, batch):
    fwd, bwd = refs[0:6], refs[6:12]
    s0_ref, mask_ref, of_ref, ob_ref, s_ref = refs[12:]

    @pl.when(pl.program_id(0) == 0)
    def _():
        s_ref[...] = s0_ref[...]

    bdmask = mask_ref[...]
    keep = bdmask.astype(F32)
    dirs = ((fwd, of_ref), (bwd, ob_ref))
    units = [(b, d, grp) for b in range(batch) for d in range(2) for grp in range(HEADS // GROUP)]
    lanes = lambda grp: slice(grp * GW, (grp + 1) * GW)

    proj = {}
    for b, d, grp in units:
        _, wk_ref, _, qd_ref, _, _ = dirs[d][0]
        lhs = jnp.concatenate([wk_ref[b, :, lanes(grp)], qd_ref[b, :, lanes(grp)]], axis=0)
        proj[b, d, grp] = _bdot(lhs, s_ref[b, d, grp])
    v_new = {}
    for b, d, grp in units:
        v_new[b, d, grp] = dirs[d][0][0][b, :, lanes(grp)].astype(F32) - proj[b, d, grp][:CHUNK]
    for b, d, grp in units:
        (_, _, qk_ref, _, kdt_ref, gt_ref), o_ref = dirs[d]
        vn = v_new[b, d, grp]
        update = _bdot(kdt_ref[b, lanes(grp), :], vn)
        s_ref[b, d, grp] = s_ref[b, d, grp] * gt_ref[b, 0, :, lanes(grp)] + update * keep
        o = proj[b, d, grp][CHUNK:] + _bdot(qk_ref[b, :, lanes(grp)], _block_diag(vn, bdmask))
        o_ref[b, :, lanes(grp)] = o.astype(o_ref.dtype)


def _delta_scan(terms, s0, bdmask):
    b, t, w = terms[0].shape
    n = t // CHUNK

    def specs(idx):
        return ([pl.BlockSpec((b, CHUNK, w), lambda i: (0, idx(i), 0))] * 4
                + [pl.BlockSpec((b, w, CHUNK), lambda i: (0, idx(i), 0)),
                   pl.BlockSpec((b, 1, 1, w), lambda i: (0, idx(i), 0, 0))])

    forward, backward = (lambda i: i), (lambda i: n - 1 - i)
    st = pl.BlockSpec(s0.shape, lambda i: (0,) * s0.ndim)
    o = jax.ShapeDtypeStruct((b, t, w), BF16)
    return pl.pallas_call(
        functools.partial(_seq_body, batch=b),
        grid=(n,),
        in_specs=specs(forward) + specs(backward) + [st, pl.BlockSpec(bdmask.shape, lambda i: (0, 0))],
        out_specs=[specs(forward)[0], specs(backward)[0], st],
        out_shape=[o, o, jax.ShapeDtypeStruct(s0.shape, F32)],
        compiler_params=pltpu.CompilerParams(dimension_semantics=("arbitrary",), vmem_limit_bytes=VMEM_LIMIT),
        name="seq",
    )(*terms, s0, bdmask)


def _merge_body(x_ref, ona_ref, gzna_ref, of_ref, ob_ref, gzdn_ref, sgna_ref, sgdn_ref, gate_ref, dnw_ref, e_ref,
                wna_ref, wdn_ref, wout_ref, o_ref):
    a = (ona_ref[0].astype(F32) * gzna_ref[0].astype(F32)).astype(BF16)
    u_na = _dot(a, wna_ref[...])
    od = of_ref[0].astype(F32) + ob_ref[0].astype(F32)
    odn = od * lax.rsqrt(_seg_sum(od * od, e_ref) * (1.0 / HEAD_DIM) + EPS) * dnw_ref[...]
    u_dn = _dot((odn * gzdn_ref[0].astype(F32)).astype(BF16), wdn_ref[...])
    y = sgna_ref[0].astype(F32) * u_na + sgdn_ref[0].astype(F32) * u_dn
    o_ref[0] = x_ref[0] + gate_ref[0] * _dot(y.astype(BF16), wout_ref[...])


def _merge(x, o_na, gz_na, o_f, o_b, gz_dn, sg_na, sg_dn, gate, dnw, e64, w_o_na, w_o_dn, w_out, tm):
    b, t, d = x.shape
    w = WIDTH
    tok = lambda width: pl.BlockSpec((1, tm, width), lambda bi, i: (bi, i, 0))
    full = lambda a: pl.BlockSpec(a.shape, lambda bi, i: (0, 0))
    return pl.pallas_call(
        _merge_body,
        grid=(b, t // tm),
        in_specs=[tok(d), tok(w), tok(w), tok(w), tok(w), tok(w), tok(d), tok(d),
                  pl.BlockSpec((1, 1, d), lambda bi, i: (bi, 0, 0)), full(dnw), full(e64),
                  full(w_o_na), full(w_o_dn), full(w_out)],
        out_specs=tok(d),
        out_shape=jax.ShapeDtypeStruct((b, t, d), F32),
        compiler_params=pltpu.CompilerParams(dimension_semantics=("parallel", "parallel"),
                                             vmem_limit_bytes=VMEM_LIMIT),
        name="merge",
    )(x, o_na, gz_na, o_f, o_b, gz_dn, sg_na, sg_dn, gate, dnw, e64, w_o_na, w_o_dn, w_out)


def _constants(tc):
    seg = np.arange(WIDTH) // HEAD_DIM
    e64 = (seg[:, None] == seg[None, :]).astype(np.float32)
    tok = np.arange(tc)
    same_chunk = (tok[:, None] // CHUNK) == (tok[None, :] // CHUNK)
    tril = (same_chunk & (tok[None, :] <= tok[:, None])).astype(np.float32)
    triu = (same_chunk & (tok[None, :] >= tok[:, None])).astype(np.float32)
    expand = np.zeros((128, 4 * WIDTH), np.float32)
    for s in range(4):
        for h in range(HEADS):
            expand[s * HEADS + h, s * WIDTH + h * HEAD_DIM:s * WIDTH + (h + 1) * HEAD_DIM] = 1.0
    lane = np.arange(GW)
    dmat = (np.arange(CHUNK)[:, None] - (lane % HEAD_DIM)[None, :]).astype(np.int32)
    bdmask = ((lane[:, None] // HEAD_DIM) == (lane[None, :] // HEAD_DIM)).astype(np.float32)
    return (jnp.asarray(e64, BF16), jnp.asarray(tril, BF16), jnp.asarray(triu, BF16), jnp.asarray(expand, BF16),
            jnp.asarray(dmat), jnp.asarray(bdmask, BF16))


def _rope_tables(t):
    half = HEAD_DIM // 4
    d = jnp.arange(2 * HEAD_DIM) % HEAD_DIM
    freqs = ROPE_BASE ** (-(d % half).astype(F32) / half)
    tok = jnp.arange(t)
    pos = jnp.where((d < HEAD_DIM // 2)[None, :], (tok // GRID_W)[:, None], (tok % GRID_W)[:, None])
    ang = pos.astype(F32) * freqs[None, :]
    cos, sin = jnp.cos(ang), jnp.sin(ang)
    lower = ((d % (2 * half)) < half)[None, :]
    return cos, jnp.where(lower, -sin, 0.0), jnp.where(lower, 0.0, sin)


def _pad_lanes(a, offset):
    return jnp.zeros((1, 128), F32).at[0, offset:offset + a.size].set(a.reshape(-1))


def _layer(x, ctx, c, c_ctx, mod_w, mod_b, norm_w, w_in, conv_w, na_q_norm, na_k_norm, na_rpb, dn_A_log, dn_dt_bias,
           dn_norm_w, w_o_na, w_o_dn, w_out):
    b, t, d = x.shape
    w = WIDTH
    tm = 256
    tc = 256
    e64, tril, triu, expand, dmat, bdmask = _constants(tc)

    cc = jnp.zeros((8, d), F32).at[:b].set(c).at[b].set(c_ctx)
    mod = _modulation(cc, mod_w, mod_b)
    shift, scale, gate = mod[:, :d], mod[:, d:2 * d], mod[:, 2 * d:]
    rows_x = lambda a: a[:b, None, :]
    rows_c = lambda a: jnp.broadcast_to(a[b][None, None, :], (b, 1, d))

    n_ba = 4 * HEADS
    ba0 = 8 * w
    w_main = jnp.concatenate([w_in[:, :ba0], w_in[:, ba0 + n_ba:]], axis=1).astype(BF16)
    w_ba = jnp.zeros((d, 128), F32).at[:, :n_ba].set(w_in[:, ba0:ba0 + n_ba]).astype(BF16)
    nw = norm_w.reshape(1, d)
    qw = jnp.tile(na_q_norm, HEADS).reshape(1, w)
    kw = jnp.tile(na_k_norm, HEADS).reshape(1, w)
    alog = _pad_lanes(dn_A_log, 2 * HEADS)
    dtb = _pad_lanes(dn_dt_bias, 2 * HEADS)
    project = functools.partial(_project, nw=nw, w_main=w_main, w_ba=w_ba, qw=qw, kw=kw, e64=e64, alog=alog,
                                dtb=dtb)
    q_na, k_na, v_na, gz_na, dn_raw, gz_dn, bg, sg_na, sg_dn = project(x, rows_x(scale), rows_x(shift), tm=2 * tm)
    _, k_c, v_c, _, dn_raw_c, _, bg_c, _, _ = project(ctx, rows_c(scale), rows_c(shift), tm=tm)

    bias = _bias_table(na_rpb, t // GRID_W)
    o_na = _neighbourhood_attention(q_na, k_na, v_na, k_c, v_c, bias)

    prepare = functools.partial(_prepare, conv_w=conv_w, e64=e64, tril=tril, triu=triu, expand=expand, tc=tc)
    pc = prepare(dn_raw_c, bg_c, tables=None)
    px = prepare(dn_raw, bg, tables=_rope_tables(t))
    s_zero = jnp.zeros((b, 2, HEADS // GROUP, GW, GW), F32)
    _, _, s_ctx = _delta_scan(_chunk_terms(*pc, dmat, bdmask, chunks=2), s_zero, bdmask)
    o_f, o_b, _ = _delta_scan(_chunk_terms(*px, dmat, bdmask, chunks=2), s_ctx, bdmask)

    dnw = jnp.tile(dn_norm_w, HEADS).reshape(1, w)
    return _merge(x, o_na, gz_na, o_f, o_b, gz_dn, sg_na, sg_dn, rows_x(gate), dnw, e64,
                  w_o_na.astype(BF16), w_o_dn.astype(BF16), w_out.astype(BF16), tm)


def kernel(x, c, ctx, c_ctx, mod_w, mod_b, norm_w, w_in, conv_w, na_q_norm, na_k_norm, na_rpb, dn_A_log, dn_dt_bias,
           dn_norm_w, w_o_na, w_o_dn, w_out):
    depth = mod_w.shape[0]
    assert depth == 1, "context-stream update between layers is not implemented"
    return _layer(x, ctx, c, c_ctx, mod_w[0], mod_b[0], norm_w[0], w_in[0], conv_w[0], na_q_norm[0], na_k_norm[0],
                  na_rpb[0], dn_A_log[0], dn_dt_bias[0], dn_norm_w[0], w_o_na[0], w_o_dn[0], w_out[0])
```

```python
import functools

import jax
import jax.numpy as jnp
import numpy as np
from jax import lax
from jax.experimental import pallas as pl
from jax.experimental.pallas import tpu as pltpu

F32 = jnp.float32
BF16 = jnp.bfloat16
EPS = 1e-6
GRID_W = 64
HEADS = 8
HEAD_DIM = 64
WIDTH = HEADS * HEAD_DIM
NA_KH = 8
NA_KW = 16
CHUNK = 64
ROPE_BASE = 10000.0
GROUP = 4
GW = GROUP * HEAD_DIM
NEG_INF = float("-inf")
HIGHEST = lax.Precision.HIGHEST
VMEM_LIMIT = 56 * 1024 * 1024


def _dot(a, b, **kw):
    return jnp.dot(a, b, preferred_element_type=F32, **kw)


def _dot_nt(a, b):
    return lax.dot_general(a, b, (((1,), (1,)), ((), ())), preferred_element_type=F32)


def _seg_sum(x, e_ref):
    return _dot(x.astype(BF16), e_ref[...])


def _silu(x):
    return x * jax.nn.sigmoid(x)


def _mod_body(c_ref, w_ref, b_ref, o_ref):
    o_ref[...] = _dot(_silu(c_ref[...]), w_ref[...], precision=HIGHEST) + b_ref[...]


def _modulation(cc, mod_w, mod_b):
    rows, d = cc.shape
    n = mod_w.shape[1]
    tn = 512
    return pl.pallas_call(
        _mod_body,
        grid=(n // tn,),
        in_specs=[pl.BlockSpec((rows, d), lambda j: (0, 0)),
                  pl.BlockSpec((d, tn), lambda j: (0, j)),
                  pl.BlockSpec((1, tn), lambda j: (0, j))],
        out_specs=pl.BlockSpec((rows, tn), lambda j: (0, j)),
        out_shape=jax.ShapeDtypeStruct((rows, n), F32),
        compiler_params=pltpu.CompilerParams(dimension_semantics=("parallel",)),
        name="mod",
    )(cc, mod_w, mod_b.reshape(1, n))


CUM_ROWS = 256


def _split2(x):
    hi = x.astype(BF16)
    return hi, (x - hi.astype(F32)).astype(BF16)


def _proj_body(*refs, rope):
    (x_ref, xp_ref, xn_ref, scale_ref, shift_ref, nw_ref, w_ref, wba_ref, qw_ref, kw_ref, e_ref, alog_ref, dtb_ref,
     cw_ref, tril_ref, triu_ref) = refs[:16]
    tabs = refs[16:19] if rope else ()
    q_ref, kt_ref, v_ref, gzna_ref, gzdn_ref, sgna_ref, sgdn_ref, dq_ref, dk_ref, dv_ref, bgc_ref = refs[-11:]
    i = pl.program_id(1)
    last = pl.num_programs(1) - 1

    def modulated(xv):
        xn = xv * lax.rsqrt(jnp.mean(xv * xv, axis=-1, keepdims=True) + EPS)
        return ((xn * nw_ref[...]) * (1.0 + scale_ref[0]) + shift_ref[0]).astype(BF16)

    hb = modulated(x_ref[0])
    tm = hb.shape[0]
    halo = modulated(jnp.concatenate([xp_ref[0, 0], xn_ref[0, 0]], axis=0))

    def mm(lo, hi):
        return _dot(hb, w_ref[:, lo:hi])

    def head_rms(a, w_row):
        return a * lax.rsqrt(_seg_sum(a * a, e_ref) * (1.0 / HEAD_DIM) + EPS) * w_row

    w = WIDTH
    q_ref[0] = (head_rms(mm(0, w), qw_ref[...]) * HEAD_DIM ** -0.5).astype(BF16)
    k_t = head_rms(mm(w, 2 * w), kw_ref[...]).T.astype(BF16)
    for j in range(kt_ref.shape[1]):
        kt_ref[0, j] = k_t[:, j * 128:(j + 1) * 128]
    v_ref[0] = mm(2 * w, 3 * w).astype(BF16)
    gzna_ref[0] = _silu(mm(3 * w, 4 * w)).astype(BF16)
    gzdn_ref[0] = _silu(mm(7 * w, 8 * w)).astype(BF16)
    sgna_ref[0] = jax.nn.sigmoid(mm(8 * w, 10 * w)).astype(BF16)
    sgdn_ref[0] = jax.nn.sigmoid(mm(10 * w, 12 * w)).astype(BF16)

    raw = _dot(jnp.concatenate([hb, halo], axis=0), w_ref[:, 4 * w:7 * w])
    xc = raw[:tm]
    before = jnp.where(i > 0, raw[tm + 7:tm + 8], 0.0)
    after = jnp.where(i < last, raw[tm + 8:tm + 9], 0.0)
    rowid = lax.broadcasted_iota(jnp.int32, xc.shape, 0)
    prev = jnp.where(rowid == 0, before, pltpu.roll(xc, 1, axis=0))
    nxt = jnp.where(rowid == tm - 1, after, pltpu.roll(xc, tm - 1, axis=0))
    y = _silu(prev * cw_ref[0:1, :] + xc * cw_ref[1:2, :] + nxt * cw_ref[2:3, :])
    half = HEAD_DIM // 4

    def norm_rope(a):
        a = a * lax.rsqrt(_seg_sum(a * a, e_ref) + EPS)
        if not rope:
            return a
        heads = lambda ref: jnp.concatenate([ref[...]] * (w // ref.shape[1]), axis=1)
        cos_ref, sina_ref, sinb_ref = tabs
        return (a * heads(cos_ref) + pltpu.roll(a, w - half, axis=1) * heads(sina_ref)
                + pltpu.roll(a, half, axis=1) * heads(sinb_ref))

    dq_ref[0] = norm_rope(y[:, :w]) * HEAD_DIM ** -0.5
    dk_ref[0] = norm_rope(y[:, w:2 * w])
    dv_ref[0] = y[:, 2 * w:]

    ba = _dot(hb, wba_ref[...])
    lane = lax.broadcasted_iota(jnp.int32, (CUM_ROWS, ba.shape[1]), 1)
    a = ba + dtb_ref[...]
    softplus = jnp.maximum(a, 0.0) + jnp.log1p(jnp.exp(-jnp.abs(a)))
    g = -jnp.exp(alog_ref[...]) * softplus
    beta = jax.nn.sigmoid(ba)
    n = ba.shape[1]
    for s in range(tm // CUM_ROWS):
        rows = slice(s * CUM_ROWS, (s + 1) * CUM_ROWS)
        parts = jnp.concatenate(_split2(g[rows]), axis=1)
        cum_f = _dot(tril_ref[...], parts)
        cum_b = _dot(triu_ref[...], parts)
        cum_f = cum_f[:, :n] + cum_f[:, n:]
        cum_b = cum_b[:, :n] + cum_b[:, n:]
        bgc_ref[0, rows] = jnp.where(lane < 2 * HEADS, beta[rows],
                                     jnp.where(lane < 3 * HEADS, cum_f, jnp.where(lane < 4 * HEADS, cum_b, 0.0)))


def _project(x, scale, shift, nw, w_main, w_ba, qw, kw, e64, alog, dtb, conv_w, tril, triu, tables, tm):
    b, t, d = x.shape
    w = WIDTH
    per8 = tm // 8
    nblk8 = t // 8
    x8 = x.reshape(b, nblk8, 8, d)
    tok = lambda width: pl.BlockSpec((1, tm, width), lambda bi, i: (bi, i, 0))
    row = lambda width: pl.BlockSpec((1, width), lambda bi, i: (0, 0))
    per_batch = pl.BlockSpec((1, 1, d), lambda bi, i: (bi, 0, 0))
    full = lambda a: pl.BlockSpec(a.shape, lambda bi, i: (0, 0))
    tab = pl.BlockSpec((tm, 2 * HEAD_DIM), lambda bi, i: (i, 0))
    sds = lambda width, dt: jax.ShapeDtypeStruct((b, t, width), dt)
    tables = tuple(tables) if tables is not None else ()
    return pl.pallas_call(
        functools.partial(_proj_body, rope=bool(tables)),
        grid=(b, t // tm),
        in_specs=[tok(d),
                  pl.BlockSpec((1, 1, 8, d), lambda bi, i: (bi, jnp.maximum(i * per8 - 1, 0), 0, 0)),
                  pl.BlockSpec((1, 1, 8, d), lambda bi, i: (bi, jnp.minimum((i + 1) * per8, nblk8 - 1), 0, 0)),
                  per_batch, per_batch, row(d), full(w_main), full(w_ba), row(w), row(w), full(e64),
                  row(128), row(128), full(conv_w), full(tril), full(triu)] + [tab] * len(tables),
        out_specs=[tok(w), pl.BlockSpec((1, tm // 128, w, 128), lambda bi, i: (bi, i, 0, 0)), tok(w), tok(w), tok(w),
                   tok(2 * w), tok(2 * w), tok(w), tok(w), tok(w), tok(128)],
        out_shape=[sds(w, BF16), jax.ShapeDtypeStruct((b, t // 128, w, 128), BF16), sds(w, BF16), sds(w, BF16),
                   sds(w, BF16), sds(2 * w, BF16), sds(2 * w, BF16), sds(w, F32), sds(w, F32), sds(w, F32),
                   sds(128, F32)],
        compiler_params=pltpu.CompilerParams(dimension_semantics=("parallel", "parallel"),
                                             vmem_limit_bytes=VMEM_LIMIT),
        name="proj",
    )(x, x8, x8, scale, shift, nw, w_main, w_ba, qw, kw, e64, alog, dtb, conv_w, tril, triu, *tables)


NA_BAND = NA_KH + 2
NA_PAIRS_PER_STEP = 4
NA_BAND_OFFSETS = NA_KH // 2 + 1


def _band_plan(rows):
    y = np.arange(NA_BAND)
    dy = np.zeros((NA_BAND_OFFSETS, 2, NA_BAND), np.int32)
    valid = np.zeros((NA_BAND_OFFSETS, 2, NA_BAND), bool)
    for var, m in enumerate((0, 1, 2, rows // 2 - 2, rows // 2 - 1)):
        b0 = min(max(2 * m - NA_KH // 2, 0), rows - NA_BAND)
        assert (2 * m - b0) // 2 == var
        for e in range(2):
            r = 2 * m + e
            r0 = min(max(r - NA_KH // 2, 0), rows - NA_KH)
            valid[var, e] = (b0 + y >= r0) & (b0 + y < r0 + NA_KH)
            dy[var, e] = np.clip(b0 + y - r + NA_KH - 1, 0, 2 * NA_KH - 2)
    return dy, valid


def _bias_body(rpb_ref, o_ref, tiles_ref, *, dy, valid):
    p = pl.program_id(0)
    n_dy, n_dx = 2 * NA_KH - 1, 2 * NA_KW - 1
    cq = lax.broadcasted_iota(jnp.int32, (GRID_W, GRID_W), 0)
    ck = lax.broadcasted_iota(jnp.int32, (GRID_W, GRID_W), 1)
    c0 = jnp.clip(cq - NA_KW // 2, 0, GRID_W - NA_KW)
    col_in = (ck >= c0) & (ck < c0 + NA_KW)
    dx = jnp.clip(ck - cq, -(NA_KW - 1), NA_KW - 1) + (NA_KW - 1)
    for hh in range(2):
        for i in range(n_dy):
            acc = jnp.zeros((GRID_W, GRID_W), F32)
            for d in range(n_dx):
                acc = jnp.where(dx == d, rpb_ref[((2 * p + hh) * n_dy + i) * n_dx + d], acc)
            tiles_ref[hh, i] = jnp.where(col_in, acc, NEG_INF)
    outside = jnp.full((GRID_W, GRID_W), NEG_INF, F32)
    for var in range(NA_BAND_OFFSETS):
        for e in range(2):
            for hh in range(2):
                r = (2 * e + hh) * GRID_W
                for y in range(NA_BAND):
                    tile = tiles_ref[hh, int(dy[var, e, y])] if valid[var, e, y] else outside
                    o_ref[0, var, r:r + GRID_W, y * GRID_W:(y + 1) * GRID_W] = tile


def _bias_table(rpb, rows):
    dy, valid = _band_plan(rows)
    shape = (HEADS // 2, NA_BAND_OFFSETS, 4 * GRID_W, NA_BAND * GRID_W)
    return pl.pallas_call(
        functools.partial(_bias_body, dy=dy, valid=valid),
        grid=(HEADS // 2,),
        in_specs=[pl.BlockSpec(memory_space=pltpu.SMEM)],
        out_specs=pl.BlockSpec((1,) + shape[1:], lambda p: (p, 0, 0, 0)),
        out_shape=jax.ShapeDtypeStruct(shape, F32),
        scratch_shapes=[pltpu.VMEM((2, 2 * NA_KH - 1, GRID_W, GRID_W), F32)],
        compiler_params=pltpu.CompilerParams(dimension_semantics=("parallel",)),
        name="bias",
    )(rpb.reshape(-1))


def _na_body(q_ref, kt_ref, v_ref, kct_ref, vc_ref, bias_ref, o_ref, *, rows):
    j = pl.program_id(2)
    lane = lax.broadcasted_iota(jnp.int32, (GRID_W, 2 * HEAD_DIM), 1)
    first = lane < HEAD_DIM
    kct = jnp.concatenate([kct_ref[0, i] for i in range(kct_ref.shape[1])], axis=1)
    vc = vc_ref[0]
    tiles_per_band = NA_BAND * GRID_W // 128
    pairs = range(NA_PAIRS_PER_STEP)

    lhs, s_win, s_ctx, start = {}, {}, {}, {}
    for mm in pairs:
        m = j * NA_PAIRS_PER_STEP + mm
        b0 = jnp.clip(2 * m - NA_KH // 2, 0, rows - NA_BAND)
        variant = (2 * m - b0) // 2
        start[mm] = b0 // 2
        parts = []
        for e in range(2):
            qr = q_ref[0, (2 * mm + e) * GRID_W:(2 * mm + e + 1) * GRID_W, :]
            zero = jnp.zeros_like(qr)
            parts += [jnp.where(first, qr, zero), jnp.where(first, zero, qr)]
        lhs[mm] = jnp.concatenate(parts, axis=0)
        kt = jnp.concatenate([kt_ref[0, start[mm] + i] for i in range(tiles_per_band)], axis=1)
        s_win[mm] = _dot(lhs[mm], kt) + bias_ref[0, variant]
        s_ctx[mm] = _dot(lhs[mm], kct)
    p_win, p_ctx, denom = {}, {}, {}
    for mm in pairs:
        peak = jnp.maximum(jnp.max(s_win[mm], axis=-1, keepdims=True), jnp.max(s_ctx[mm], axis=-1, keepdims=True))
        p_win[mm] = jnp.exp(s_win[mm] - peak)
        p_ctx[mm] = jnp.exp(s_ctx[mm] - peak)
        denom[mm] = jnp.sum(p_win[mm], axis=-1, keepdims=True) + jnp.sum(p_ctx[mm], axis=-1, keepdims=True)
    for mm in pairs:
        vb = v_ref[0, pl.ds(pl.multiple_of(start[mm] * 128, 128), NA_BAND * GRID_W), :]
        o = (_dot(p_win[mm].astype(BF16), vb) + _dot(p_ctx[mm].astype(BF16), vc)) / denom[mm]
        for e in range(2):
            top = o[2 * e * GRID_W:(2 * e + 1) * GRID_W]
            bottom = o[(2 * e + 1) * GRID_W:(2 * e + 2) * GRID_W]
            o_ref[0, (2 * mm + e) * GRID_W:(2 * mm + e + 1) * GRID_W, :] = jnp.where(first, top, bottom).astype(
                o_ref.dtype)


def _neighbourhood_attention(q, kt, v, kct, vc, bias):
    b, t, _ = q.shape
    ctx_len = vc.shape[1]
    rows = t // GRID_W
    rows_per_step = 2 * NA_PAIRS_PER_STEP
    assert rows >= NA_BAND + 2 and rows % rows_per_step == 0
    tq = rows_per_step * GRID_W
    pair_w = 2 * HEAD_DIM
    return pl.pallas_call(
        functools.partial(_na_body, rows=rows),
        grid=(b, HEADS // 2, t // tq),
        in_specs=[pl.BlockSpec((1, tq, pair_w), lambda bi, p, j: (bi, j, p)),
                  pl.BlockSpec((1, t // 128, pair_w, 128), lambda bi, p, j: (bi, 0, p, 0)),
                  pl.BlockSpec((1, t, pair_w), lambda bi, p, j: (bi, 0, p)),
                  pl.BlockSpec((1, ctx_len // 128, pair_w, 128), lambda bi, p, j: (bi, 0, p, 0)),
                  pl.BlockSpec((1, ctx_len, pair_w), lambda bi, p, j: (bi, 0, p)),
                  pl.BlockSpec((1,) + bias.shape[1:], lambda bi, p, j: (p, 0, 0, 0))],
        out_specs=pl.BlockSpec((1, tq, pair_w), lambda bi, p, j: (bi, j, p)),
        out_shape=jax.ShapeDtypeStruct((b, t, WIDTH), BF16),
        compiler_params=pltpu.CompilerParams(dimension_semantics=("parallel", "parallel", "arbitrary"),
                                             vmem_limit_bytes=VMEM_LIMIT),
        name="na",
    )(q, kt, v, kct, vc, bias)


def _block_diag(x, mask):
    return jnp.concatenate([x.astype(BF16)] * GROUP, axis=0) * mask


def _bdot(a, b):
    return _dot(a.astype(BF16), b.astype(BF16))


def _chunk_body(q_ref, k_ref, v_ref, bgc_ref, ex_ref, dmat_ref, mask_ref, *out_refs, chunks):
    dmat = dmat_ref[...]
    bdmask = mask_ref[...]
    eye = (dmat == 0).astype(F32)
    n_out = len(out_refs) // 2
    bd = lambda x: _block_diag(x, bdmask)
    tile = lambda ref, c, grp: ref[0, c * CHUNK:(c + 1) * CHUNK, grp * GW:(grp + 1) * GW]
    tiles = [(c, grp) for c in range(chunks) for grp in range(HEADS // GROUP)]
    units = [(c, grp, d) for c, grp in tiles for d in range(2)]

    hi, lo = _split2(bgc_ref[0])
    tc = hi.shape[0]
    beta_wide = _dot(hi, ex_ref[:, :2 * WIDTH])
    gc_wide = _dot(jnp.concatenate([hi, lo], axis=0), ex_ref[:, 2 * WIDTH:])
    gc_wide = gc_wide[:tc] + gc_wide[tc:]
    wide = lambda a, c, grp, d: a[c * CHUNK:(c + 1) * CHUNK, d * WIDTH + grp * GW:d * WIDTH + (grp + 1) * GW]

    both = {t: _dot_nt(jnp.concatenate([tile(k_ref, *t), tile(q_ref, *t)], axis=0).astype(BF16), bd(tile(k_ref, *t)))
            for t in tiles}

    row = lax.broadcasted_iota(jnp.int32, dmat.shape, 0)
    col = row - dmat
    same16, same32 = (row // 16) == (col // 16), (row // 32) == (col // 32)
    qk, inv, diag, off32, off64 = {}, {}, {}, {}, {}
    for c, grp, d in units:
        beta, gc = wide(beta_wide, c, grp, d), wide(gc_wide, c, grp, d)
        incl, strict = (dmat <= 0, dmat < 0) if d else (dmat >= 0, dmat > 0)
        g_row = jnp.sum(gc * eye, axis=0, keepdims=True)
        decay = jnp.exp(jnp.where(incl, gc - g_row, NEG_INF))
        kk_qk = both[c, grp]
        low = jnp.where(strict, kk_qk[:CHUNK] * beta * decay, 0.0)
        qk[c, grp, d] = kk_qk[CHUNK:] * decay
        diag[c, grp, d] = jnp.where(same16, low, 0.0)
        off32[c, grp, d] = jnp.where(same32 & ~same16, low, 0.0)
        off64[c, grp, d] = jnp.where(same32, 0.0, low)
        inv[c, grp, d] = eye - diag[c, grp, d]

    power = {un: _bdot(diag[un], bd(diag[un])) for un in units}
    for _ in range(2):
        res = {un: _bdot(jnp.concatenate([power[un], inv[un]], axis=0), bd(power[un])) for un in units}
        power = {un: res[un][:CHUNK] for un in units}
        inv = {un: inv[un] + res[un][CHUNK:] for un in units}
    inv = {un: inv[un] + _bdot(inv[un], bd(power[un])) for un in units}
    for off in (off32, off64):
        half = {un: _bdot(inv[un], bd(off[un])) for un in units}
        inv = {un: inv[un] - _bdot(half[un], bd(inv[un])) for un in units}

    for c, grp, d in units:
        un = (c, grp, d)
        q, k, v = tile(q_ref, c, grp), tile(k_ref, c, grp), tile(v_ref, c, grp)
        beta, gc = wide(beta_wide, c, grp, d), wide(gc_wide, c, grp, d)
        last = 0 if d else CHUNK - 1
        eg = jnp.exp(gc)
        g_last = gc[last:last + 1, :]
        u = _bdot(inv[un], bd(v * beta))
        wk = _bdot(inv[un], bd(k * beta * eg))
        k_dec = k * jnp.exp(g_last - gc)
        u_ref, wk_ref, qk_ref, qd_ref, kdt_ref, gt_ref = out_refs[d * n_out:(d + 1) * n_out]
        rows, sl = slice(c * CHUNK, (c + 1) * CHUNK), slice(grp * GW, (grp + 1) * GW)
        u_ref[0, rows, sl] = u.astype(BF16)
        wk_ref[0, rows, sl] = wk.astype(BF16)
        qk_ref[0, rows, sl] = qk[un].astype(BF16)
        qd_ref[0, rows, sl] = (q * eg).astype(BF16)
        k_dec_t = k_dec.T.astype(BF16)
        for h in range(GROUP):
            kdt_ref[0, rows, grp * GW + h * HEAD_DIM:grp * GW + (h + 1) * HEAD_DIM] = (
                k_dec_t[h * HEAD_DIM:(h + 1) * HEAD_DIM, :])
        gt_ref[0, c, :, sl] = jnp.exp(g_last)


def _chunk_terms(q, k, v, bgc, expand, dmat, bdmask, chunks):
    b, t, w = q.shape
    n = t // CHUNK
    tc = chunks * CHUNK
    tok = pl.BlockSpec((1, tc, w), lambda bi, i: (bi, i, 0))
    full = lambda a: pl.BlockSpec(a.shape, lambda bi, i: (0, 0))
    per_dir_specs = [tok] * 5 + [pl.BlockSpec((1, chunks, 1, w), lambda bi, i: (bi, i, 0, 0))]
    act = jax.ShapeDtypeStruct((b, t, w), BF16)
    per_dir_shapes = [act] * 5 + [jax.ShapeDtypeStruct((b, n, 1, w), F32)]
    return pl.pallas_call(
        functools.partial(_chunk_body, chunks=chunks),
        grid=(b, t // tc),
        in_specs=[tok] * 3 + [pl.BlockSpec((1, tc, 128), lambda bi, i: (bi, i, 0)), full(expand), full(dmat),
                  full(bdmask)],
        out_specs=per_dir_specs * 2,
        out_shape=per_dir_shapes * 2,
        compiler_params=pltpu.CompilerParams(dimension_semantics=("parallel", "parallel"),
                                             vmem_limit_bytes=VMEM_LIMIT),
        name="chunk",
    )(q, k, v, bgc, expand, dmat, bdmask)


def _seq_body(*refs, batch):
    fwd, bwd = refs[0:6], refs[6:12]
    s0_ref, mask_ref, of_ref, ob_ref, s_ref = refs[12:]

    @pl.when(pl.program_id(0) == 0)
    def _():
        s_ref[...] = s0_ref[...]

    bdmask = mask_ref[...]
    dirs = ((fwd, of_ref), (bwd, ob_ref))
    units = [(b, d, grp) for b in range(batch) for d in range(2) for grp in range(HEADS // GROUP)]
    lanes = lambda grp: slice(grp * GW, (grp + 1) * GW)

    proj = {}
    for b, d, grp in units:
        _, wk_ref, _, qd_ref, _, _ = dirs[d][0]
        lhs = jnp.concatenate([wk_ref[b, :, lanes(grp)], qd_ref[b, :, lanes(grp)]], axis=0)
        proj[b, d, grp] = _dot(lhs, _block_diag(s_ref[b, d, grp], bdmask))
    for b, d, grp in units:
        (u_ref, _, qk_ref, _, kdt_ref, gt_ref), o_ref = dirs[d]
        v_new = u_ref[b, :, lanes(grp)].astype(F32) - proj[b, d, grp][:CHUNK]
        lhs = jnp.concatenate([qk_ref[b, :, lanes(grp)], kdt_ref[b, :, lanes(grp)]], axis=0)
        res = _dot(lhs, _block_diag(v_new, bdmask))
        o_ref[b, :, lanes(grp)] = (proj[b, d, grp][CHUNK:] + res[:CHUNK]).astype(o_ref.dtype)
        s_ref[b, d, grp] = s_ref[b, d, grp] * gt_ref[b, 0, :, lanes(grp)] + res[CHUNK:]


def _delta_scan(terms, s0, bdmask):
    b, t, w = terms[0].shape
    n = t // CHUNK

    def specs(idx):
        return ([pl.BlockSpec((b, CHUNK, w), lambda i: (0, idx(i), 0))] * 5
                + [pl.BlockSpec((b, 1, 1, w), lambda i: (0, idx(i), 0, 0))])

    forward, backward = (lambda i: i), (lambda i: n - 1 - i)
    st = pl.BlockSpec(s0.shape, lambda i: (0,) * s0.ndim)
    o = jax.ShapeDtypeStruct((b, t, w), BF16)
    return pl.pallas_call(
        functools.partial(_seq_body, batch=b),
        grid=(n,),
        in_specs=specs(forward) + specs(backward) + [st, pl.BlockSpec(bdmask.shape, lambda i: (0, 0))],
        out_specs=[specs(forward)[0], specs(backward)[0], st],
        out_shape=[o, o, jax.ShapeDtypeStruct(s0.shape, F32)],
        compiler_params=pltpu.CompilerParams(dimension_semantics=("arbitrary",), vmem_limit_bytes=VMEM_LIMIT),
        name="seq",
    )(*terms, s0, bdmask)


def _merge_body(x_ref, ona_ref, gzna_ref, of_ref, ob_ref, gzdn_ref, sgna_ref, sgdn_ref, gate_ref, dnw_ref, e_ref,
                wna_ref, wdn_ref, wout_ref, o_ref):
    a = (ona_ref[0].astype(F32) * gzna_ref[0].astype(F32)).astype(BF16)
    u_na = _dot(a, wna_ref[...])
    od = of_ref[0].astype(F32) + ob_ref[0].astype(F32)
    odn = od * lax.rsqrt(_seg_sum(od * od, e_ref) * (1.0 / HEAD_DIM) + EPS) * dnw_ref[...]
    u_dn = _dot((odn * gzdn_ref[0].astype(F32)).astype(BF16), wdn_ref[...])
    y = sgna_ref[0].astype(F32) * u_na + sgdn_ref[0].astype(F32) * u_dn
    o_ref[0] = x_ref[0] + gate_ref[0] * _dot(y.astype(BF16), wout_ref[...])


def _merge(x, o_na, gz_na, o_f, o_b, gz_dn, sg_na, sg_dn, gate, dnw, e64, w_o_na, w_o_dn, w_out, tm):
    b, t, d = x.shape
    w = WIDTH
    tok = lambda width: pl.BlockSpec((1, tm, width), lambda bi, i: (bi, i, 0))
    full = lambda a: pl.BlockSpec(a.shape, lambda bi, i: (0, 0))
    return pl.pallas_call(
        _merge_body,
        grid=(b, t // tm),
        in_specs=[tok(d), tok(w), tok(w), tok(w), tok(w), tok(w), tok(d), tok(d),
                  pl.BlockSpec((1, 1, d), lambda bi, i: (bi, 0, 0)), full(dnw), full(e64),
                  full(w_o_na), full(w_o_dn), full(w_out)],
        out_specs=tok(d),
        out_shape=jax.ShapeDtypeStruct((b, t, d), F32),
        compiler_params=pltpu.CompilerParams(dimension_semantics=("parallel", "parallel"),
                                             vmem_limit_bytes=VMEM_LIMIT),
        name="merge",
    )(x, o_na, gz_na, o_f, o_b, gz_dn, sg_na, sg_dn, gate, dnw, e64, w_o_na, w_o_dn, w_out)


def _constants(tc):
    seg = np.arange(WIDTH) // HEAD_DIM
    e64 = (seg[:, None] == seg[None, :]).astype(np.float32)
    tok = np.arange(tc)
    same_chunk = (tok[:, None] // CHUNK) == (tok[None, :] // CHUNK)
    tril = (same_chunk & (tok[None, :] <= tok[:, None])).astype(np.float32)
    triu = (same_chunk & (tok[None, :] >= tok[:, None])).astype(np.float32)
    expand = np.zeros((128, 4 * WIDTH), np.float32)
    for s in range(4):
        for h in range(HEADS):
            expand[s * HEADS + h, s * WIDTH + h * HEAD_DIM:s * WIDTH + (h + 1) * HEAD_DIM] = 1.0
    lane = np.arange(GW)
    dmat = (np.arange(CHUNK)[:, None] - (lane % HEAD_DIM)[None, :]).astype(np.int32)
    bdmask = ((lane[:, None] // HEAD_DIM) == (lane[None, :] // HEAD_DIM)).astype(np.float32)
    return (jnp.asarray(e64, BF16), jnp.asarray(tril, BF16), jnp.asarray(triu, BF16), jnp.asarray(expand, BF16),
            jnp.asarray(dmat), jnp.asarray(bdmask, BF16))


def _rope_tables(t):
    half = HEAD_DIM // 4
    d = jnp.arange(2 * HEAD_DIM) % HEAD_DIM
    freqs = ROPE_BASE ** (-(d % half).astype(F32) / half)
    tok = jnp.arange(t)
    pos = jnp.where((d < HEAD_DIM // 2)[None, :], (tok // GRID_W)[:, None], (tok % GRID_W)[:, None])
    ang = pos.astype(F32) * freqs[None, :]
    cos, sin = jnp.cos(ang), jnp.sin(ang)
    lower = ((d % (2 * half)) < half)[None, :]
    return cos, jnp.where(lower, -sin, 0.0), jnp.where(lower, 0.0, sin)


def _pad_lanes(a, offset):
    return jnp.zeros((1, 128), F32).at[0, offset:offset + a.size].set(a.reshape(-1))


def _layer(x, ctx, c, c_ctx, mod_w, mod_b, norm_w, w_in, conv_w, na_q_norm, na_k_norm, na_rpb, dn_A_log, dn_dt_bias,
           dn_norm_w, w_o_na, w_o_dn, w_out):
    b, t, d = x.shape
    w = WIDTH
    tm = 256
    e64, tril, triu, expand, dmat, bdmask = _constants(CUM_ROWS)

    cc = jnp.zeros((8, d), F32).at[:b].set(c).at[b].set(c_ctx)
    mod = _modulation(cc, mod_w, mod_b)
    shift, scale, gate = mod[:, :d], mod[:, d:2 * d], mod[:, 2 * d:]
    rows_x = lambda a: a[:b, None, :]
    rows_c = lambda a: jnp.broadcast_to(a[b][None, None, :], (b, 1, d))

    n_ba = 4 * HEADS
    ba0 = 8 * w
    w_main = jnp.concatenate([w_in[:, :ba0], w_in[:, ba0 + n_ba:]], axis=1).astype(BF16)
    w_ba = jnp.zeros((d, 128), F32).at[:, :n_ba].set(w_in[:, ba0:ba0 + n_ba]).astype(BF16)
    nw = norm_w.reshape(1, d)
    qw = jnp.tile(na_q_norm, HEADS).reshape(1, w)
    kw = jnp.tile(na_k_norm, HEADS).reshape(1, w)
    alog = _pad_lanes(dn_A_log, 2 * HEADS)
    dtb = _pad_lanes(dn_dt_bias, 2 * HEADS)
    project = functools.partial(_project, nw=nw, w_main=w_main, w_ba=w_ba, qw=qw, kw=kw, e64=e64, alog=alog,
                                dtb=dtb, conv_w=conv_w, tril=tril, triu=triu)
    q_na, k_na, v_na, gz_na, gz_dn, sg_na, sg_dn, *dn = project(x, rows_x(scale), rows_x(shift),
                                                                 tables=_rope_tables(t), tm=2 * tm)
    _, k_c, v_c, _, _, _, _, *dn_c = project(ctx, rows_c(scale), rows_c(shift), tables=None, tm=tm)

    bias = _bias_table(na_rpb, t // GRID_W)
    o_na = _neighbourhood_attention(q_na, k_na, v_na, k_c, v_c, bias)

    s_zero = jnp.zeros((b, 2, HEADS // GROUP, HEAD_DIM, GW), F32)
    _, _, s_ctx = _delta_scan(_chunk_terms(*dn_c, expand, dmat, bdmask, chunks=2), s_zero, bdmask)
    o_f, o_b, _ = _delta_scan(_chunk_terms(*dn, expand, dmat, bdmask, chunks=2), s_ctx, bdmask)

    dnw = jnp.tile(dn_norm_w, HEADS).reshape(1, w)
    return _merge(x, o_na, gz_na, o_f, o_b, gz_dn, sg_na, sg_dn, rows_x(gate), dnw, e64,
                  w_o_na.astype(BF16), w_o_dn.astype(BF16), w_out.astype(BF16), tm)


def kernel(x, c, ctx, c_ctx, mod_w, mod_b, norm_w, w_in, conv_w, na_q_norm, na_k_norm, na_rpb, dn_A_log, dn_dt_bias,
           dn_norm_w, w_o_na, w_o_dn, w_out):
    depth = mod_w.shape[0]
    assert depth == 1, "context-stream update between layers is not implemented"
    return _layer(x, ctx, c, c_ctx, mod_w[0], mod_b[0], norm_w[0], w_in[0], conv_w[0], na_q_norm[0], na_k_norm[0],
                  na_rpb[0], dn_A_log[0], dn_dt_bias[0], dn_norm_w[0], w_o_na[0], w_o_dn[0], w_out[0])
```

```python
import functools

import jax
import jax.numpy as jnp
import numpy as np
from jax import lax
from jax.experimental import pallas as pl
from jax.experimental.pallas import tpu as pltpu

F32 = jnp.float32
BF16 = jnp.bfloat16
EPS = 1e-6
GRID_W = 64
HEADS = 8
HEAD_DIM = 64
WIDTH = HEADS * HEAD_DIM
NA_KH = 8
NA_KW = 16
CHUNK = 64
ROPE_BASE = 10000.0
GROUP = 4
GW = GROUP * HEAD_DIM
NEG_INF = float("-inf")
HIGHEST = lax.Precision.HIGHEST
VMEM_LIMIT = 56 * 1024 * 1024


def _dot(a, b, **kw):
    return jnp.dot(a, b, preferred_element_type=F32, **kw)


def _dot_nt(a, b):
    return lax.dot_general(a, b, (((1,), (1,)), ((), ())), preferred_element_type=F32)


def _seg_sum(x, e_ref):
    xb = x.astype(BF16)
    e = e_ref[...]
    return jnp.concatenate([_dot(xb[:, g * GW:(g + 1) * GW], e) for g in range(x.shape[1] // GW)], axis=1)


def _silu(x):
    return x * jax.nn.sigmoid(x)


def _mod_body(c_ref, w_ref, b_ref, o_ref):
    o_ref[...] = _dot(_silu(c_ref[...]), w_ref[...], precision=HIGHEST) + b_ref[...]


def _modulation(cc, mod_w, mod_b):
    rows, d = cc.shape
    n = mod_w.shape[1]
    tn = 512
    return pl.pallas_call(
        _mod_body,
        grid=(n // tn,),
        in_specs=[pl.BlockSpec((rows, d), lambda j: (0, 0)),
                  pl.BlockSpec((d, tn), lambda j: (0, j)),
                  pl.BlockSpec((1, tn), lambda j: (0, j))],
        out_specs=pl.BlockSpec((rows, tn), lambda j: (0, j)),
        out_shape=jax.ShapeDtypeStruct((rows, n), F32),
        compiler_params=pltpu.CompilerParams(dimension_semantics=("parallel",)),
        name="mod",
    )(cc, mod_w, mod_b.reshape(1, n))


CUM_ROWS = 256


def _split2(x):
    hi = x.astype(BF16)
    return hi, (x - hi.astype(F32)).astype(BF16)


def _proj_body(*refs, rope):
    (x_ref, xp_ref, xn_ref, scale_ref, shift_ref, nw_ref, w_ref, wba_ref, qw_ref, kw_ref, e_ref, alog_ref, dtb_ref,
     cw_ref, tril_ref, triu_ref) = refs[:16]
    tabs = refs[16:19] if rope else ()
    q_ref, kt_ref, v_ref, gzna_ref, gzdn_ref, sgna_ref, sgdn_ref, dq_ref, dk_ref, dv_ref, bgc_ref = refs[-11:]
    i = pl.program_id(1)
    last = pl.num_programs(1) - 1

    def modulated(xv):
        xn = xv * lax.rsqrt(jnp.mean(xv * xv, axis=-1, keepdims=True) + EPS)
        return ((xn * nw_ref[...]) * (1.0 + scale_ref[0]) + shift_ref[0]).astype(BF16)

    hb = modulated(x_ref[0])
    tm = hb.shape[0]
    halo = modulated(jnp.concatenate([xp_ref[0, 0], xn_ref[0, 0]], axis=0))

    def mm(lo, hi):
        return _dot(hb, w_ref[:, lo:hi])

    def head_rms(a, w_row):
        return a * lax.rsqrt(_seg_sum(a * a, e_ref) * (1.0 / HEAD_DIM) + EPS) * w_row

    w = WIDTH
    q_ref[0] = (head_rms(mm(0, w), qw_ref[...]) * HEAD_DIM ** -0.5).astype(BF16)
    k_t = head_rms(mm(w, 2 * w), kw_ref[...]).T.astype(BF16)
    for j in range(kt_ref.shape[1]):
        kt_ref[0, j] = k_t[:, j * 128:(j + 1) * 128]
    v_ref[0] = mm(2 * w, 3 * w).astype(BF16)
    gzna_ref[0] = _silu(mm(3 * w, 4 * w)).astype(BF16)
    gzdn_ref[0] = _silu(mm(7 * w, 8 * w)).astype(BF16)
    sgna_ref[0] = jax.nn.sigmoid(mm(8 * w, 10 * w)).astype(BF16)
    sgdn_ref[0] = jax.nn.sigmoid(mm(10 * w, 12 * w)).astype(BF16)

    raw = _dot(jnp.concatenate([hb, halo], axis=0), w_ref[:, 4 * w:7 * w])
    xc = raw[:tm]
    before = jnp.where(i > 0, raw[tm + 7:tm + 8], 0.0)
    after = jnp.where(i < last, raw[tm + 8:tm + 9], 0.0)
    rowid = lax.broadcasted_iota(jnp.int32, xc.shape, 0)
    prev = jnp.where(rowid == 0, before, pltpu.roll(xc, 1, axis=0))
    nxt = jnp.where(rowid == tm - 1, after, pltpu.roll(xc, tm - 1, axis=0))
    y = _silu(prev * cw_ref[0:1, :] + xc * cw_ref[1:2, :] + nxt * cw_ref[2:3, :])
    half = HEAD_DIM // 4

    def norm_rope(a):
        a = a * lax.rsqrt(_seg_sum(a * a, e_ref) + EPS)
        if not rope:
            return a
        heads = lambda ref: jnp.concatenate([ref[...]] * (w // ref.shape[1]), axis=1)
        cos_ref, sina_ref, sinb_ref = tabs
        return (a * heads(cos_ref) + pltpu.roll(a, w - half, axis=1) * heads(sina_ref)
                + pltpu.roll(a, half, axis=1) * heads(sinb_ref))

    dq_ref[0] = norm_rope(y[:, :w]) * HEAD_DIM ** -0.5
    dk_ref[0] = norm_rope(y[:, w:2 * w])
    dv_ref[0] = y[:, 2 * w:]

    ba = _dot(hb, wba_ref[...])
    lane = lax.broadcasted_iota(jnp.int32, (CUM_ROWS, ba.shape[1]), 1)
    a = ba + dtb_ref[...]
    softplus = jnp.maximum(a, 0.0) + jnp.log1p(jnp.exp(-jnp.abs(a)))
    g = -jnp.exp(alog_ref[...]) * softplus
    beta = jax.nn.sigmoid(ba)
    n = ba.shape[1]
    for s in range(tm // CUM_ROWS):
        rows = slice(s * CUM_ROWS, (s + 1) * CUM_ROWS)
        parts = jnp.concatenate(_split2(g[rows]), axis=1)
        cum_f = _dot(tril_ref[...], parts)
        cum_b = _dot(triu_ref[...], parts)
        cum_f = cum_f[:, :n] + cum_f[:, n:]
        cum_b = cum_b[:, :n] + cum_b[:, n:]
        bgc_ref[0, rows] = jnp.where(lane < 2 * HEADS, beta[rows],
                                     jnp.where(lane < 3 * HEADS, cum_f, jnp.where(lane < 4 * HEADS, cum_b, 0.0)))


def _project(x, scale, shift, nw, w_main, w_ba, qw, kw, e64, alog, dtb, conv_w, tril, triu, tables, tm):
    b, t, d = x.shape
    w = WIDTH
    per8 = tm // 8
    nblk8 = t // 8
    x8 = x.reshape(b, nblk8, 8, d)
    tok = lambda width: pl.BlockSpec((1, tm, width), lambda bi, i: (bi, i, 0))
    row = lambda width: pl.BlockSpec((1, width), lambda bi, i: (0, 0))
    per_batch = pl.BlockSpec((1, 1, d), lambda bi, i: (bi, 0, 0))
    full = lambda a: pl.BlockSpec(a.shape, lambda bi, i: (0, 0))
    tab = pl.BlockSpec((tm, 2 * HEAD_DIM), lambda bi, i: (i, 0))
    sds = lambda width, dt: jax.ShapeDtypeStruct((b, t, width), dt)
    tables = tuple(tables) if tables is not None else ()
    return pl.pallas_call(
        functools.partial(_proj_body, rope=bool(tables)),
        grid=(b, t // tm),
        in_specs=[tok(d),
                  pl.BlockSpec((1, 1, 8, d), lambda bi, i: (bi, jnp.maximum(i * per8 - 1, 0), 0, 0)),
                  pl.BlockSpec((1, 1, 8, d), lambda bi, i: (bi, jnp.minimum((i + 1) * per8, nblk8 - 1), 0, 0)),
                  per_batch, per_batch, row(d), full(w_main), full(w_ba), row(w), row(w), full(e64),
                  row(128), row(128), full(conv_w), full(tril), full(triu)] + [tab] * len(tables),
        out_specs=[tok(w), pl.BlockSpec((1, tm // 128, w, 128), lambda bi, i: (bi, i, 0, 0)), tok(w), tok(w), tok(w),
                   tok(2 * w), tok(2 * w), tok(w), tok(w), tok(w), tok(128)],
        out_shape=[sds(w, BF16), jax.ShapeDtypeStruct((b, t // 128, w, 128), BF16), sds(w, BF16), sds(w, BF16),
                   sds(w, BF16), sds(2 * w, BF16), sds(2 * w, BF16), sds(w, F32), sds(w, F32), sds(w, F32),
                   sds(128, F32)],
        compiler_params=pltpu.CompilerParams(dimension_semantics=("parallel", "parallel"),
                                             vmem_limit_bytes=VMEM_LIMIT),
        name="proj",
    )(x, x8, x8, scale, shift, nw, w_main, w_ba, qw, kw, e64, alog, dtb, conv_w, tril, triu, *tables)


NA_BAND = NA_KH + 2
NA_PAIRS_PER_STEP = 8
NA_BAND_OFFSETS = NA_KH // 2 + 1


def _band_plan(rows):
    y = np.arange(NA_BAND)
    dy = np.zeros((NA_BAND_OFFSETS, 2, NA_BAND), np.int32)
    valid = np.zeros((NA_BAND_OFFSETS, 2, NA_BAND), bool)
    for var, m in enumerate((0, 1, 2, rows // 2 - 2, rows // 2 - 1)):
        b0 = min(max(2 * m - NA_KH // 2, 0), rows - NA_BAND)
        assert (2 * m - b0) // 2 == var
        for e in range(2):
            r = 2 * m + e
            r0 = min(max(r - NA_KH // 2, 0), rows - NA_KH)
            valid[var, e] = (b0 + y >= r0) & (b0 + y < r0 + NA_KH)
            dy[var, e] = np.clip(b0 + y - r + NA_KH - 1, 0, 2 * NA_KH - 2)
    return dy, valid


def _bias_body(rpb_ref, o_ref, tiles_ref, *, dy, valid):
    p = pl.program_id(0)
    n_dy, n_dx = 2 * NA_KH - 1, 2 * NA_KW - 1
    cq = lax.broadcasted_iota(jnp.int32, (GRID_W, GRID_W), 0)
    ck = lax.broadcasted_iota(jnp.int32, (GRID_W, GRID_W), 1)
    c0 = jnp.clip(cq - NA_KW // 2, 0, GRID_W - NA_KW)
    col_in = (ck >= c0) & (ck < c0 + NA_KW)
    dx = jnp.clip(ck - cq, -(NA_KW - 1), NA_KW - 1) + (NA_KW - 1)
    for hh in range(2):
        for i in range(n_dy):
            acc = jnp.zeros((GRID_W, GRID_W), F32)
            for d in range(n_dx):
                acc = jnp.where(dx == d, rpb_ref[((2 * p + hh) * n_dy + i) * n_dx + d], acc)
            tiles_ref[hh, i] = jnp.where(col_in, acc, NEG_INF)
    outside = jnp.full((GRID_W, GRID_W), NEG_INF, F32)
    for var in range(NA_BAND_OFFSETS):
        for e in range(2):
            for hh in range(2):
                r = (2 * e + hh) * GRID_W
                for y in range(NA_BAND):
                    tile = tiles_ref[hh, int(dy[var, e, y])] if valid[var, e, y] else outside
                    o_ref[0, var, r:r + GRID_W, y * GRID_W:(y + 1) * GRID_W] = tile


def _bias_table(rpb, rows):
    dy, valid = _band_plan(rows)
    shape = (HEADS // 2, NA_BAND_OFFSETS, 4 * GRID_W, NA_BAND * GRID_W)
    return pl.pallas_call(
        functools.partial(_bias_body, dy=dy, valid=valid),
        grid=(HEADS // 2,),
        in_specs=[pl.BlockSpec(memory_space=pltpu.SMEM)],
        out_specs=pl.BlockSpec((1,) + shape[1:], lambda p: (p, 0, 0, 0)),
        out_shape=jax.ShapeDtypeStruct(shape, F32),
        scratch_shapes=[pltpu.VMEM((2, 2 * NA_KH - 1, GRID_W, GRID_W), F32)],
        compiler_params=pltpu.CompilerParams(dimension_semantics=("parallel",)),
        name="bias",
    )(rpb.reshape(-1))


def _na_body(q_ref, kt_ref, v_ref, kct_ref, vc_ref, bias_ref, o_ref, *, rows):
    j = pl.program_id(2)
    lane = lax.broadcasted_iota(jnp.int32, (GRID_W, 2 * HEAD_DIM), 1)
    first = lane < HEAD_DIM
    kct = jnp.concatenate([kct_ref[0, i] for i in range(kct_ref.shape[1])], axis=1)
    vc = vc_ref[0]
    tiles_per_band = NA_BAND * GRID_W // 128
    pairs = range(NA_PAIRS_PER_STEP)

    lhs, s_win, s_ctx, start = {}, {}, {}, {}
    for mm in pairs:
        m = j * NA_PAIRS_PER_STEP + mm
        b0 = jnp.clip(2 * m - NA_KH // 2, 0, rows - NA_BAND)
        variant = (2 * m - b0) // 2
        start[mm] = b0 // 2
        parts = []
        for e in range(2):
            qr = q_ref[0, (2 * mm + e) * GRID_W:(2 * mm + e + 1) * GRID_W, :]
            zero = jnp.zeros_like(qr)
            parts += [jnp.where(first, qr, zero), jnp.where(first, zero, qr)]
        lhs[mm] = jnp.concatenate(parts, axis=0)
        kt = jnp.concatenate([kt_ref[0, start[mm] + i] for i in range(tiles_per_band)], axis=1)
        s_win[mm] = _dot(lhs[mm], kt) + bias_ref[0, variant]
        s_ctx[mm] = _dot(lhs[mm], kct)
    p_win, p_ctx, denom = {}, {}, {}
    for mm in pairs:
        peak = jnp.maximum(jnp.max(s_win[mm], axis=-1, keepdims=True), jnp.max(s_ctx[mm], axis=-1, keepdims=True))
        p_win[mm] = jnp.exp(s_win[mm] - peak)
        p_ctx[mm] = jnp.exp(s_ctx[mm] - peak)
        denom[mm] = jnp.sum(p_win[mm], axis=-1, keepdims=True) + jnp.sum(p_ctx[mm], axis=-1, keepdims=True)
    for mm in pairs:
        vb = v_ref[0, pl.ds(pl.multiple_of(start[mm] * 128, 128), NA_BAND * GRID_W), :]
        o = (_dot(p_win[mm].astype(BF16), vb) + _dot(p_ctx[mm].astype(BF16), vc)) / denom[mm]
        for e in range(2):
            top = o[2 * e * GRID_W:(2 * e + 1) * GRID_W]
            bottom = o[(2 * e + 1) * GRID_W:(2 * e + 2) * GRID_W]
            o_ref[0, (2 * mm + e) * GRID_W:(2 * mm + e + 1) * GRID_W, :] = jnp.where(first, top, bottom).astype(
                o_ref.dtype)


def _neighbourhood_attention(q, kt, v, kct, vc, bias):
    b, t, _ = q.shape
    ctx_len = vc.shape[1]
    rows = t // GRID_W
    rows_per_step = 2 * NA_PAIRS_PER_STEP
    assert rows >= NA_BAND + 2 and rows % rows_per_step == 0
    tq = rows_per_step * GRID_W
    pair_w = 2 * HEAD_DIM
    return pl.pallas_call(
        functools.partial(_na_body, rows=rows),
        grid=(b, HEADS // 2, t // tq),
        in_specs=[pl.BlockSpec((1, tq, pair_w), lambda bi, p, j: (bi, j, p)),
                  pl.BlockSpec((1, t // 128, pair_w, 128), lambda bi, p, j: (bi, 0, p, 0)),
                  pl.BlockSpec((1, t, pair_w), lambda bi, p, j: (bi, 0, p)),
                  pl.BlockSpec((1, ctx_len // 128, pair_w, 128), lambda bi, p, j: (bi, 0, p, 0)),
                  pl.BlockSpec((1, ctx_len, pair_w), lambda bi, p, j: (bi, 0, p)),
                  pl.BlockSpec((1,) + bias.shape[1:], lambda bi, p, j: (p, 0, 0, 0))],
        out_specs=pl.BlockSpec((1, tq, pair_w), lambda bi, p, j: (bi, j, p)),
        out_shape=jax.ShapeDtypeStruct((b, t, WIDTH), BF16),
        compiler_params=pltpu.CompilerParams(dimension_semantics=("parallel", "parallel", "arbitrary"),
                                             vmem_limit_bytes=VMEM_LIMIT),
        name="na",
    )(q, kt, v, kct, vc, bias)


def _block_diag(x, mask):
    return jnp.concatenate([x.astype(BF16)] * GROUP, axis=0) * mask


def _bdot(a, b):
    return _dot(a.astype(BF16), b.astype(BF16))


def _chunk_body(q_ref, k_ref, v_ref, bgc_ref, ex_ref, dmat_ref, mask_ref, *out_refs, chunks):
    dmat = dmat_ref[...]
    bdmask = mask_ref[...]
    eye = (dmat == 0).astype(F32)
    n_out = len(out_refs) // 2
    bd = lambda x: _block_diag(x, bdmask)
    tile = lambda ref, c, grp: ref[0, c * CHUNK:(c + 1) * CHUNK, grp * GW:(grp + 1) * GW]
    tiles = [(c, grp) for c in range(chunks) for grp in range(HEADS // GROUP)]
    units = [(c, grp, d) for c, grp in tiles for d in range(2)]

    hi, lo = _split2(bgc_ref[0])
    tc = hi.shape[0]
    beta_wide = _dot(hi, ex_ref[:, :2 * WIDTH])
    gc_wide = _dot(jnp.concatenate([hi, lo], axis=0), ex_ref[:, 2 * WIDTH:])
    gc_wide = gc_wide[:tc] + gc_wide[tc:]
    wide = lambda a, c, grp, d: a[c * CHUNK:(c + 1) * CHUNK, d * WIDTH + grp * GW:d * WIDTH + (grp + 1) * GW]

    both = {t: _dot_nt(jnp.concatenate([tile(k_ref, *t), tile(q_ref, *t)], axis=0).astype(BF16), bd(tile(k_ref, *t)))
            for t in tiles}

    row = lax.broadcasted_iota(jnp.int32, dmat.shape, 0)
    col = row - dmat
    same16, same32 = (row // 16) == (col // 16), (row // 32) == (col // 32)
    qk, inv, diag, off32, off64 = {}, {}, {}, {}, {}
    for c, grp, d in units:
        beta, gc = wide(beta_wide, c, grp, d), wide(gc_wide, c, grp, d)
        incl, strict = (dmat <= 0, dmat < 0) if d else (dmat >= 0, dmat > 0)
        g_row = jnp.sum(gc * eye, axis=0, keepdims=True)
        decay = jnp.exp(jnp.where(incl, gc - g_row, NEG_INF))
        kk_qk = both[c, grp]
        low = jnp.where(strict, kk_qk[:CHUNK] * beta * decay, 0.0)
        qk[c, grp, d] = kk_qk[CHUNK:] * decay
        diag[c, grp, d] = jnp.where(same16, low, 0.0)
        off32[c, grp, d] = jnp.where(same32 & ~same16, low, 0.0)
        off64[c, grp, d] = jnp.where(same32, 0.0, low)
        inv[c, grp, d] = eye - diag[c, grp, d]

    power = {un: _bdot(diag[un], bd(diag[un])) for un in units}
    for _ in range(2):
        res = {un: _bdot(jnp.concatenate([power[un], inv[un]], axis=0), bd(power[un])) for un in units}
        power = {un: res[un][:CHUNK] for un in units}
        inv = {un: inv[un] + res[un][CHUNK:] for un in units}
    inv = {un: inv[un] + _bdot(inv[un], bd(power[un])) for un in units}
    for off in (off32, off64):
        half = {un: _bdot(inv[un], bd(off[un])) for un in units}
        inv = {un: inv[un] - _bdot(half[un], bd(inv[un])) for un in units}

    for c, grp, d in units:
        un = (c, grp, d)
        q, k, v = tile(q_ref, c, grp), tile(k_ref, c, grp), tile(v_ref, c, grp)
        beta, gc = wide(beta_wide, c, grp, d), wide(gc_wide, c, grp, d)
        last = 0 if d else CHUNK - 1
        eg = jnp.exp(gc)
        g_last = gc[last:last + 1, :]
        u = _bdot(inv[un], bd(v * beta))
        wk = _bdot(inv[un], bd(k * beta * eg))
        k_dec = k * jnp.exp(g_last - gc)
        u_ref, wk_ref, qk_ref, qd_ref, kdt_ref, gt_ref = out_refs[d * n_out:(d + 1) * n_out]
        rows, sl = slice(c * CHUNK, (c + 1) * CHUNK), slice(grp * GW, (grp + 1) * GW)
        u_ref[0, rows, sl] = u.astype(BF16)
        wk_ref[0, rows, sl] = wk.astype(BF16)
        qk_ref[0, rows, sl] = qk[un].astype(BF16)
        qd_ref[0, rows, sl] = (q * eg).astype(BF16)
        k_dec_t = k_dec.T.astype(BF16)
        for h in range(GROUP):
            kdt_ref[0, rows, grp * GW + h * HEAD_DIM:grp * GW + (h + 1) * HEAD_DIM] = (
                k_dec_t[h * HEAD_DIM:(h + 1) * HEAD_DIM, :])
        gt_ref[0, c, :, sl] = jnp.exp(g_last)


def _chunk_terms(q, k, v, bgc, expand, dmat, bdmask, chunks):
    b, t, w = q.shape
    n = t // CHUNK
    tc = chunks * CHUNK
    tok = pl.BlockSpec((1, tc, w), lambda bi, i: (bi, i, 0))
    full = lambda a: pl.BlockSpec(a.shape, lambda bi, i: (0, 0))
    per_dir_specs = [tok] * 5 + [pl.BlockSpec((1, chunks, 1, w), lambda bi, i: (bi, i, 0, 0))]
    act = jax.ShapeDtypeStruct((b, t, w), BF16)
    per_dir_shapes = [act] * 5 + [jax.ShapeDtypeStruct((b, n, 1, w), F32)]
    return pl.pallas_call(
        functools.partial(_chunk_body, chunks=chunks),
        grid=(b, t // tc),
        in_specs=[tok] * 3 + [pl.BlockSpec((1, tc, 128), lambda bi, i: (bi, i, 0)), full(expand), full(dmat),
                  full(bdmask)],
        out_specs=per_dir_specs * 2,
        out_shape=per_dir_shapes * 2,
        compiler_params=pltpu.CompilerParams(dimension_semantics=("parallel", "parallel"),
                                             vmem_limit_bytes=VMEM_LIMIT),
        name="chunk",
    )(q, k, v, bgc, expand, dmat, bdmask)


SEQ_CHUNKS = 4


def _seq_body(*refs, batch):
    fwd, bwd = refs[0:6], refs[6:12]
    s0_ref, mask_ref, of_ref, ob_ref, s_ref = refs[12:]

    @pl.when(pl.program_id(0) == 0)
    def _():
        s_ref[...] = s0_ref[...]

    bdmask = mask_ref[...]
    dirs = ((fwd, of_ref), (bwd, ob_ref))
    units = [(b, d, grp) for b in range(batch) for d in range(2) for grp in range(HEADS // GROUP)]
    lanes = lambda grp: slice(grp * GW, (grp + 1) * GW)

    for sub in range(SEQ_CHUNKS):
        chunk_of = (sub, SEQ_CHUNKS - 1 - sub)
        rows = [slice(c * CHUNK, (c + 1) * CHUNK) for c in chunk_of]
        proj = {}
        for b, d, grp in units:
            _, wk_ref, _, qd_ref, _, _ = dirs[d][0]
            lhs = jnp.concatenate([wk_ref[b, rows[d], lanes(grp)], qd_ref[b, rows[d], lanes(grp)]], axis=0)
            proj[b, d, grp] = _dot(lhs, _block_diag(s_ref[b, d, grp], bdmask))
        for b, d, grp in units:
            (u_ref, _, qk_ref, _, kdt_ref, gt_ref), o_ref = dirs[d]
            v_new = u_ref[b, rows[d], lanes(grp)].astype(F32) - proj[b, d, grp][:CHUNK]
            lhs = jnp.concatenate([qk_ref[b, rows[d], lanes(grp)], kdt_ref[b, rows[d], lanes(grp)]], axis=0)
            res = _dot(lhs, _block_diag(v_new, bdmask))
            o_ref[b, rows[d], lanes(grp)] = (proj[b, d, grp][CHUNK:] + res[:CHUNK]).astype(o_ref.dtype)
            s_ref[b, d, grp] = s_ref[b, d, grp] * gt_ref[b, chunk_of[d], :, lanes(grp)] + res[CHUNK:]


def _delta_scan(terms, s0, bdmask):
    b, t, w = terms[0].shape
    steps = t // (SEQ_CHUNKS * CHUNK)

    def specs(idx):
        return ([pl.BlockSpec((b, SEQ_CHUNKS * CHUNK, w), lambda i: (0, idx(i), 0))] * 5
                + [pl.BlockSpec((b, SEQ_CHUNKS, 1, w), lambda i: (0, idx(i), 0, 0))])

    forward, backward = (lambda i: i), (lambda i: steps - 1 - i)
    st = pl.BlockSpec(s0.shape, lambda i: (0,) * s0.ndim)
    o = jax.ShapeDtypeStruct((b, t, w), BF16)
    return pl.pallas_call(
        functools.partial(_seq_body, batch=b),
        grid=(steps,),
        in_specs=specs(forward) + specs(backward) + [st, pl.BlockSpec(bdmask.shape, lambda i: (0, 0))],
        out_specs=[specs(forward)[0], specs(backward)[0], st],
        out_shape=[o, o, jax.ShapeDtypeStruct(s0.shape, F32)],
        compiler_params=pltpu.CompilerParams(dimension_semantics=("arbitrary",), vmem_limit_bytes=VMEM_LIMIT),
        name="seq",
    )(*terms, s0, bdmask)


def _merge_body(x_ref, ona_ref, gzna_ref, of_ref, ob_ref, gzdn_ref, sgna_ref, sgdn_ref, gate_ref, dnw_ref, e_ref,
                wna_ref, wdn_ref, wout_ref, o_ref):
    a = (ona_ref[0].astype(F32) * gzna_ref[0].astype(F32)).astype(BF16)
    u_na = _dot(a, wna_ref[...])
    od = of_ref[0].astype(F32) + ob_ref[0].astype(F32)
    odn = od * lax.rsqrt(_seg_sum(od * od, e_ref) * (1.0 / HEAD_DIM) + EPS) * dnw_ref[...]
    u_dn = _dot((odn * gzdn_ref[0].astype(F32)).astype(BF16), wdn_ref[...])
    y = sgna_ref[0].astype(F32) * u_na + sgdn_ref[0].astype(F32) * u_dn
    o_ref[0] = x_ref[0] + gate_ref[0] * _dot(y.astype(BF16), wout_ref[...])


def _merge(x, o_na, gz_na, o_f, o_b, gz_dn, sg_na, sg_dn, gate, dnw, e64, w_o_na, w_o_dn, w_out, tm):
    b, t, d = x.shape
    w = WIDTH
    tok = lambda width: pl.BlockSpec((1, tm, width), lambda bi, i: (bi, i, 0))
    full = lambda a: pl.BlockSpec(a.shape, lambda bi, i: (0, 0))
    return pl.pallas_call(
        _merge_body,
        grid=(b, t // tm),
        in_specs=[tok(d), tok(w), tok(w), tok(w), tok(w), tok(w), tok(d), tok(d),
                  pl.BlockSpec((1, 1, d), lambda bi, i: (bi, 0, 0)), full(dnw), full(e64),
                  full(w_o_na), full(w_o_dn), full(w_out)],
        out_specs=tok(d),
        out_shape=jax.ShapeDtypeStruct((b, t, d), F32),
        compiler_params=pltpu.CompilerParams(dimension_semantics=("parallel", "parallel"),
                                             vmem_limit_bytes=VMEM_LIMIT),
        name="merge",
    )(x, o_na, gz_na, o_f, o_b, gz_dn, sg_na, sg_dn, gate, dnw, e64, w_o_na, w_o_dn, w_out)


def _constants(tc):
    seg = np.arange(GW) // HEAD_DIM
    e64 = (seg[:, None] == seg[None, :]).astype(np.float32)
    tok = np.arange(tc)
    same_chunk = (tok[:, None] // CHUNK) == (tok[None, :] // CHUNK)
    tril = (same_chunk & (tok[None, :] <= tok[:, None])).astype(np.float32)
    triu = (same_chunk & (tok[None, :] >= tok[:, None])).astype(np.float32)
    expand = np.zeros((128, 4 * WIDTH), np.float32)
    for s in range(4):
        for h in range(HEADS):
            expand[s * HEADS + h, s * WIDTH + h * HEAD_DIM:s * WIDTH + (h + 1) * HEAD_DIM] = 1.0
    lane = np.arange(GW)
    dmat = (np.arange(CHUNK)[:, None] - (lane % HEAD_DIM)[None, :]).astype(np.int32)
    bdmask = ((lane[:, None] // HEAD_DIM) == (lane[None, :] // HEAD_DIM)).astype(np.float32)
    return (jnp.asarray(e64, BF16), jnp.asarray(tril, BF16), jnp.asarray(triu, BF16), jnp.asarray(expand, BF16),
            jnp.asarray(dmat), jnp.asarray(bdmask, BF16))


def _rope_tables(t):
    half = HEAD_DIM // 4
    d = jnp.arange(2 * HEAD_DIM) % HEAD_DIM
    freqs = ROPE_BASE ** (-(d % half).astype(F32) / half)
    tok = jnp.arange(t)
    pos = jnp.where((d < HEAD_DIM // 2)[None, :], (tok // GRID_W)[:, None], (tok % GRID_W)[:, None])
    ang = pos.astype(F32) * freqs[None, :]
    cos, sin = jnp.cos(ang), jnp.sin(ang)
    lower = ((d % (2 * half)) < half)[None, :]
    return cos, jnp.where(lower, -sin, 0.0), jnp.where(lower, 0.0, sin)


def _pad_lanes(a, offset):
    return jnp.zeros((1, 128), F32).at[0, offset:offset + a.size].set(a.reshape(-1))


def _layer(x, ctx, c, c_ctx, mod_w, mod_b, norm_w, w_in, conv_w, na_q_norm, na_k_norm, na_rpb, dn_A_log, dn_dt_bias,
           dn_norm_w, w_o_na, w_o_dn, w_out):
    b, t, d = x.shape
    w = WIDTH
    tm = 256
    e64, tril, triu, expand, dmat, bdmask = _constants(CUM_ROWS)

    cc = jnp.zeros((8, d), F32).at[:b].set(c).at[b].set(c_ctx)
    mod = _modulation(cc, mod_w, mod_b)
    shift, scale, gate = mod[:, :d], mod[:, d:2 * d], mod[:, 2 * d:]
    rows_x = lambda a: a[:b, None, :]
    rows_c = lambda a: jnp.broadcast_to(a[b][None, None, :], (b, 1, d))

    n_ba = 4 * HEADS
    ba0 = 8 * w
    w_main = jnp.concatenate([w_in[:, :ba0], w_in[:, ba0 + n_ba:]], axis=1).astype(BF16)
    w_ba = jnp.zeros((d, 128), F32).at[:, :n_ba].set(w_in[:, ba0:ba0 + n_ba]).astype(BF16)
    nw = norm_w.reshape(1, d)
    qw = jnp.tile(na_q_norm, HEADS).reshape(1, w)
    kw = jnp.tile(na_k_norm, HEADS).reshape(1, w)
    alog = _pad_lanes(dn_A_log, 2 * HEADS)
    dtb = _pad_lanes(dn_dt_bias, 2 * HEADS)
    project = functools.partial(_project, nw=nw, w_main=w_main, w_ba=w_ba, qw=qw, kw=kw, e64=e64, alog=alog,
                                dtb=dtb, conv_w=conv_w, tril=tril, triu=triu)
    q_na, k_na, v_na, gz_na, gz_dn, sg_na, sg_dn, *dn = project(x, rows_x(scale), rows_x(shift),
                                                                 tables=_rope_tables(t), tm=2 * tm)
    _, k_c, v_c, _, _, _, _, *dn_c = project(ctx, rows_c(scale), rows_c(shift), tables=None, tm=tm)

    bias = _bias_table(na_rpb, t // GRID_W)
    o_na = _neighbourhood_attention(q_na, k_na, v_na, k_c, v_c, bias)

    s_zero = jnp.zeros((b, 2, HEADS // GROUP, HEAD_DIM, GW), F32)
    _, _, s_ctx = _delta_scan(_chunk_terms(*dn_c, expand, dmat, bdmask, chunks=2), s_zero, bdmask)
    o_f, o_b, _ = _delta_scan(_chunk_terms(*dn, expand, dmat, bdmask, chunks=4), s_ctx, bdmask)

    dnw = jnp.tile(dn_norm_w, HEADS).reshape(1, w)
    return _merge(x, o_na, gz_na, o_f, o_b, gz_dn, sg_na, sg_dn, rows_x(gate), dnw, e64,
                  w_o_na.astype(BF16), w_o_dn.astype(BF16), w_out.astype(BF16), 2 * tm)


def kernel(x, c, ctx, c_ctx, mod_w, mod_b, norm_w, w_in, conv_w, na_q_norm, na_k_norm, na_rpb, dn_A_log, dn_dt_bias,
           dn_norm_w, w_o_na, w_o_dn, w_out):
    depth = mod_w.shape[0]
    assert depth == 1, "context-stream update between layers is not implemented"
    return _layer(x, ctx, c, c_ctx, mod_w[0], mod_b[0], norm_w[0], w_in[0], conv_w[0], na_q_norm[0], na_k_norm[0],
                  na_rpb[0], dn_A_log[0], dn_dt_bias[0], dn_norm_w[0], w_o_na[0], w_o_dn[0], w_out[0])
```

```python
import functools

import jax
import jax.numpy as jnp
import numpy as np
from jax import lax
from jax.experimental import pallas as pl
from jax.experimental.pallas import tpu as pltpu

F32 = jnp.float32
BF16 = jnp.bfloat16
EPS = 1e-6
GRID_W = 64
HEADS = 8
HEAD_DIM = 64
WIDTH = HEADS * HEAD_DIM
NA_KH = 8
NA_KW = 16
CHUNK = 64
ROPE_BASE = 10000.0
GROUP = 4
GW = GROUP * HEAD_DIM
NEG_INF = float("-inf")
LOG2E = 1.4426950408889634
HIGHEST = lax.Precision.HIGHEST
VMEM_LIMIT = 56 * 1024 * 1024


def _dot(a, b, **kw):
    return jnp.dot(a, b, preferred_element_type=F32, **kw)


def _dot_nt(a, b):
    return lax.dot_general(a, b, (((1,), (1,)), ((), ())), preferred_element_type=F32)


def _seg_sum(x, e_ref):
    xb = x.astype(BF16)
    e = e_ref[...]
    return jnp.concatenate([_dot(xb[:, g * GW:(g + 1) * GW], e) for g in range(x.shape[1] // GW)], axis=1)


def _silu(x):
    return x * jax.nn.sigmoid(x)


def _mod_body(c_ref, w_ref, b_ref, o_ref):
    o_ref[...] = _dot(_silu(c_ref[...]), w_ref[...], precision=HIGHEST) + b_ref[...]


def _modulation(cc, mod_w, mod_b):
    rows, d = cc.shape
    n = mod_w.shape[1]
    tn = 512
    return pl.pallas_call(
        _mod_body,
        grid=(n // tn,),
        in_specs=[pl.BlockSpec((rows, d), lambda j: (0, 0)),
                  pl.BlockSpec((d, tn), lambda j: (0, j)),
                  pl.BlockSpec((1, tn), lambda j: (0, j))],
        out_specs=pl.BlockSpec((rows, tn), lambda j: (0, j)),
        out_shape=jax.ShapeDtypeStruct((rows, n), F32),
        compiler_params=pltpu.CompilerParams(dimension_semantics=("parallel",)),
        name="mod",
    )(cc, mod_w, mod_b.reshape(1, n))


CUM_ROWS = 256


def _split2(x):
    hi = x.astype(BF16)
    return hi, (x - hi.astype(F32)).astype(BF16)


def _proj_body(*refs, rope):
    (x_ref, xp_ref, xn_ref, scale_ref, shift_ref, nw_ref, w_ref, wba_ref, qw_ref, kw_ref, e_ref, alog_ref, dtb_ref,
     cw_ref, tril_ref, triu_ref) = refs[:16]
    tabs = refs[16:19] if rope else ()
    q_ref, kt_ref, v_ref, gzna_ref, gzdn_ref, sgna_ref, sgdn_ref, dq_ref, dk_ref, dv_ref, bgc_ref = refs[-11:]
    i = pl.program_id(1)
    last = pl.num_programs(1) - 1

    def modulated(xv):
        xn = xv * lax.rsqrt(jnp.mean(xv * xv, axis=-1, keepdims=True) + EPS)
        return ((xn * nw_ref[...]) * (1.0 + scale_ref[0]) + shift_ref[0]).astype(BF16)

    hb = modulated(x_ref[0])
    tm = hb.shape[0]
    halo = modulated(jnp.concatenate([xp_ref[0, 0], xn_ref[0, 0]], axis=0))

    def mm(lo, hi):
        return _dot(hb, w_ref[:, lo:hi])

    def head_rms(a, w_row):
        return a * lax.rsqrt(_seg_sum(a * a, e_ref) * (1.0 / HEAD_DIM) + EPS) * w_row

    w = WIDTH
    q_ref[0] = (head_rms(mm(0, w), qw_ref[...]) * (HEAD_DIM ** -0.5 * LOG2E)).astype(BF16)
    k_t = head_rms(mm(w, 2 * w), kw_ref[...]).T.astype(BF16)
    for j in range(kt_ref.shape[1]):
        kt_ref[0, j] = k_t[:, j * 128:(j + 1) * 128]
    v_ref[0] = mm(2 * w, 3 * w).astype(BF16)
    gzna_ref[0] = _silu(mm(3 * w, 4 * w)).astype(BF16)
    gzdn_ref[0] = _silu(mm(7 * w, 8 * w)).astype(BF16)
    sgna_ref[0] = jax.nn.sigmoid(mm(8 * w, 10 * w)).astype(BF16)
    sgdn_ref[0] = jax.nn.sigmoid(mm(10 * w, 12 * w)).astype(BF16)

    raw = _dot(jnp.concatenate([hb, halo], axis=0), w_ref[:, 4 * w:7 * w])
    xc = raw[:tm]
    before = jnp.where(i > 0, raw[tm + 7:tm + 8], 0.0)
    after = jnp.where(i < last, raw[tm + 8:tm + 9], 0.0)
    rowid = lax.broadcasted_iota(jnp.int32, xc.shape, 0)
    prev = jnp.where(rowid == 0, before, pltpu.roll(xc, 1, axis=0))
    nxt = jnp.where(rowid == tm - 1, after, pltpu.roll(xc, tm - 1, axis=0))
    y = _silu(prev * cw_ref[0:1, :] + xc * cw_ref[1:2, :] + nxt * cw_ref[2:3, :])
    half = HEAD_DIM // 4

    def norm_rope(a):
        a = a * lax.rsqrt(_seg_sum(a * a, e_ref) + EPS)
        if not rope:
            return a
        heads = lambda ref: jnp.concatenate([ref[...]] * (w // ref.shape[1]), axis=1)
        cos_ref, sina_ref, sinb_ref = tabs
        return (a * heads(cos_ref) + pltpu.roll(a, w - half, axis=1) * heads(sina_ref)
                + pltpu.roll(a, half, axis=1) * heads(sinb_ref))

    dq_ref[0] = norm_rope(y[:, :w]) * HEAD_DIM ** -0.5
    dk_ref[0] = norm_rope(y[:, w:2 * w])
    dv_ref[0] = y[:, 2 * w:]

    ba = _dot(hb, wba_ref[...])
    lane = lax.broadcasted_iota(jnp.int32, (CUM_ROWS, ba.shape[1]), 1)
    a = ba + dtb_ref[...]
    softplus = jnp.maximum(a, 0.0) + jnp.log1p(jnp.exp(-jnp.abs(a)))
    g = -jnp.exp(alog_ref[...]) * softplus
    beta = jax.nn.sigmoid(ba)
    n = ba.shape[1]
    for s in range(tm // CUM_ROWS):
        rows = slice(s * CUM_ROWS, (s + 1) * CUM_ROWS)
        parts = jnp.concatenate(_split2(g[rows]), axis=1)
        cum_f = _dot(tril_ref[...], parts)
        cum_b = _dot(triu_ref[...], parts)
        cum_f = (cum_f[:, :n] + cum_f[:, n:]) * LOG2E
        cum_b = (cum_b[:, :n] + cum_b[:, n:]) * LOG2E
        bgc_ref[0, rows] = jnp.where(lane < 2 * HEADS, beta[rows],
                                     jnp.where(lane < 3 * HEADS, cum_f, jnp.where(lane < 4 * HEADS, cum_b, 0.0)))


def _project(x, scale, shift, nw, w_main, w_ba, qw, kw, e64, alog, dtb, conv_w, tril, triu, tables, tm):
    b, t, d = x.shape
    w = WIDTH
    per8 = tm // 8
    nblk8 = t // 8
    x8 = x.reshape(b, nblk8, 8, d)
    tok = lambda width: pl.BlockSpec((1, tm, width), lambda bi, i: (bi, i, 0))
    row = lambda width: pl.BlockSpec((1, width), lambda bi, i: (0, 0))
    per_batch = pl.BlockSpec((1, 1, d), lambda bi, i: (bi, 0, 0))
    full = lambda a: pl.BlockSpec(a.shape, lambda bi, i: (0, 0))
    tab = pl.BlockSpec((tm, 2 * HEAD_DIM), lambda bi, i: (i, 0))
    sds = lambda width, dt: jax.ShapeDtypeStruct((b, t, width), dt)
    tables = tuple(tables) if tables is not None else ()
    return pl.pallas_call(
        functools.partial(_proj_body, rope=bool(tables)),
        grid=(b, t // tm),
        in_specs=[tok(d),
                  pl.BlockSpec((1, 1, 8, d), lambda bi, i: (bi, jnp.maximum(i * per8 - 1, 0), 0, 0)),
                  pl.BlockSpec((1, 1, 8, d), lambda bi, i: (bi, jnp.minimum((i + 1) * per8, nblk8 - 1), 0, 0)),
                  per_batch, per_batch, row(d), full(w_main), full(w_ba), row(w), row(w), full(e64),
                  row(128), row(128), full(conv_w), full(tril), full(triu)] + [tab] * len(tables),
        out_specs=[tok(w), pl.BlockSpec((1, tm // 128, w, 128), lambda bi, i: (bi, i, 0, 0)), tok(w), tok(w), tok(w),
                   tok(2 * w), tok(2 * w), tok(w), tok(w), tok(w), tok(128)],
        out_shape=[sds(w, BF16), jax.ShapeDtypeStruct((b, t // 128, w, 128), BF16), sds(w, BF16), sds(w, BF16),
                   sds(w, BF16), sds(2 * w, BF16), sds(2 * w, BF16), sds(w, F32), sds(w, F32), sds(w, F32),
                   sds(128, F32)],
        compiler_params=pltpu.CompilerParams(dimension_semantics=("parallel", "parallel"),
                                             vmem_limit_bytes=VMEM_LIMIT),
        name="proj",
    )(x, x8, x8, scale, shift, nw, w_main, w_ba, qw, kw, e64, alog, dtb, conv_w, tril, triu, *tables)


NA_BAND = NA_KH + 2
NA_PAIRS_PER_STEP = 8
NA_BAND_OFFSETS = NA_KH // 2 + 1


def _band_plan(rows):
    y = np.arange(NA_BAND)
    dy = np.zeros((NA_BAND_OFFSETS, 2, NA_BAND), np.int32)
    valid = np.zeros((NA_BAND_OFFSETS, 2, NA_BAND), bool)
    for var, m in enumerate((0, 1, 2, rows // 2 - 2, rows // 2 - 1)):
        b0 = min(max(2 * m - NA_KH // 2, 0), rows - NA_BAND)
        assert (2 * m - b0) // 2 == var
        for e in range(2):
            r = 2 * m + e
            r0 = min(max(r - NA_KH // 2, 0), rows - NA_KH)
            valid[var, e] = (b0 + y >= r0) & (b0 + y < r0 + NA_KH)
            dy[var, e] = np.clip(b0 + y - r + NA_KH - 1, 0, 2 * NA_KH - 2)
    return dy, valid


def _bias_body(rpb_ref, o_ref, tiles_ref, *, dy, valid):
    p = pl.program_id(0)
    n_dy, n_dx = 2 * NA_KH - 1, 2 * NA_KW - 1
    cq = lax.broadcasted_iota(jnp.int32, (GRID_W, GRID_W), 0)
    ck = lax.broadcasted_iota(jnp.int32, (GRID_W, GRID_W), 1)
    c0 = jnp.clip(cq - NA_KW // 2, 0, GRID_W - NA_KW)
    col_in = (ck >= c0) & (ck < c0 + NA_KW)
    dx = jnp.clip(ck - cq, -(NA_KW - 1), NA_KW - 1) + (NA_KW - 1)
    for hh in range(2):
        for i in range(n_dy):
            acc = jnp.zeros((GRID_W, GRID_W), F32)
            for d in range(n_dx):
                acc = jnp.where(dx == d, rpb_ref[((2 * p + hh) * n_dy + i) * n_dx + d], acc)
            tiles_ref[hh, i] = jnp.where(col_in, acc * LOG2E, NEG_INF)
    outside = jnp.full((GRID_W, GRID_W), NEG_INF, F32)
    for var in range(NA_BAND_OFFSETS):
        for e in range(2):
            for hh in range(2):
                r = (2 * e + hh) * GRID_W
                for y in range(NA_BAND):
                    tile = tiles_ref[hh, int(dy[var, e, y])] if valid[var, e, y] else outside
                    o_ref[0, var, r:r + GRID_W, y * GRID_W:(y + 1) * GRID_W] = tile


def _bias_table(rpb, rows):
    dy, valid = _band_plan(rows)
    shape = (HEADS // 2, NA_BAND_OFFSETS, 4 * GRID_W, NA_BAND * GRID_W)
    return pl.pallas_call(
        functools.partial(_bias_body, dy=dy, valid=valid),
        grid=(HEADS // 2,),
        in_specs=[pl.BlockSpec(memory_space=pltpu.SMEM)],
        out_specs=pl.BlockSpec((1,) + shape[1:], lambda p: (p, 0, 0, 0)),
        out_shape=jax.ShapeDtypeStruct(shape, F32),
        scratch_shapes=[pltpu.VMEM((2, 2 * NA_KH - 1, GRID_W, GRID_W), F32)],
        compiler_params=pltpu.CompilerParams(dimension_semantics=("parallel",)),
        name="bias",
    )(rpb.reshape(-1))


def _na_body(q_ref, kt_ref, v_ref, kct_ref, vc_ref, bias_ref, o_ref, *, rows):
    j = pl.program_id(2)
    lane = lax.broadcasted_iota(jnp.int32, (GRID_W, 2 * HEAD_DIM), 1)
    first = lane < HEAD_DIM
    kct = jnp.concatenate([kct_ref[0, i] for i in range(kct_ref.shape[1])], axis=1)
    with_ones = lambda v: jnp.concatenate([v, jnp.ones_like(v)], axis=1)
    vc = with_ones(vc_ref[0])
    tiles_per_band = NA_BAND * GRID_W // 128
    pairs = range(NA_PAIRS_PER_STEP)

    lhs, s_win, s_ctx, start = {}, {}, {}, {}
    for mm in pairs:
        m = j * NA_PAIRS_PER_STEP + mm
        b0 = jnp.clip(2 * m - NA_KH // 2, 0, rows - NA_BAND)
        variant = (2 * m - b0) // 2
        start[mm] = b0 // 2
        parts = []
        for e in range(2):
            qr = q_ref[0, (2 * mm + e) * GRID_W:(2 * mm + e + 1) * GRID_W, :]
            zero = jnp.zeros_like(qr)
            parts += [jnp.where(first, qr, zero), jnp.where(first, zero, qr)]
        lhs[mm] = jnp.concatenate(parts, axis=0)
        kt = jnp.concatenate([kt_ref[0, start[mm] + i] for i in range(tiles_per_band)], axis=1)
        s_win[mm] = _dot(lhs[mm], kt) + bias_ref[0, variant]
        s_ctx[mm] = _dot(lhs[mm], kct)
    p_win, p_ctx = {}, {}
    for mm in pairs:
        peak = jnp.maximum(jnp.max(s_win[mm], axis=-1, keepdims=True), jnp.max(s_ctx[mm], axis=-1, keepdims=True))
        p_win[mm] = jnp.exp2(s_win[mm] - peak).astype(BF16)
        p_ctx[mm] = jnp.exp2(s_ctx[mm] - peak).astype(BF16)
    for mm in pairs:
        vb = with_ones(v_ref[0, pl.ds(pl.multiple_of(start[mm] * 128, 128), NA_BAND * GRID_W), :])
        o = _dot(p_win[mm], vb) + _dot(p_ctx[mm], vc)
        o = o[:, :2 * HEAD_DIM] / o[:, 2 * HEAD_DIM:]
        for e in range(2):
            top = o[2 * e * GRID_W:(2 * e + 1) * GRID_W]
            bottom = o[(2 * e + 1) * GRID_W:(2 * e + 2) * GRID_W]
            o_ref[0, (2 * mm + e) * GRID_W:(2 * mm + e + 1) * GRID_W, :] = jnp.where(first, top, bottom).astype(
                o_ref.dtype)


def _neighbourhood_attention(q, kt, v, kct, vc, bias):
    b, t, _ = q.shape
    ctx_len = vc.shape[1]
    rows = t // GRID_W
    rows_per_step = 2 * NA_PAIRS_PER_STEP
    assert rows >= NA_BAND + 2 and rows % rows_per_step == 0
    tq = rows_per_step * GRID_W
    pair_w = 2 * HEAD_DIM
    return pl.pallas_call(
        functools.partial(_na_body, rows=rows),
        grid=(b, HEADS // 2, t // tq),
        in_specs=[pl.BlockSpec((1, tq, pair_w), lambda bi, p, j: (bi, j, p)),
                  pl.BlockSpec((1, t // 128, pair_w, 128), lambda bi, p, j: (bi, 0, p, 0)),
                  pl.BlockSpec((1, t, pair_w), lambda bi, p, j: (bi, 0, p)),
                  pl.BlockSpec((1, ctx_len // 128, pair_w, 128), lambda bi, p, j: (bi, 0, p, 0)),
                  pl.BlockSpec((1, ctx_len, pair_w), lambda bi, p, j: (bi, 0, p)),
                  pl.BlockSpec((1,) + bias.shape[1:], lambda bi, p, j: (p, 0, 0, 0))],
        out_specs=pl.BlockSpec((1, tq, pair_w), lambda bi, p, j: (bi, j, p)),
        out_shape=jax.ShapeDtypeStruct((b, t, WIDTH), BF16),
        compiler_params=pltpu.CompilerParams(dimension_semantics=("parallel", "parallel", "arbitrary"),
                                             vmem_limit_bytes=VMEM_LIMIT),
        name="na",
    )(q, kt, v, kct, vc, bias)


def _block_diag(x, mask):
    return jnp.concatenate([x.astype(BF16)] * GROUP, axis=0) * mask


def _bdot(a, b):
    return _dot(a.astype(BF16), b.astype(BF16))


def _chunk_body(q_ref, k_ref, v_ref, bgc_ref, ex_ref, dmat_ref, mask_ref, *out_refs, chunks):
    dmat = dmat_ref[...]
    bdmask = mask_ref[...]
    eye = (dmat == 0).astype(F32)
    n_out = len(out_refs) // 2
    bd = lambda x: _block_diag(x, bdmask)
    tile = lambda ref, c, grp: ref[0, c * CHUNK:(c + 1) * CHUNK, grp * GW:(grp + 1) * GW]
    tiles = [(c, grp) for c in range(chunks) for grp in range(HEADS // GROUP)]
    units = [(c, grp, d) for c, grp in tiles for d in range(2)]

    hi, lo = _split2(bgc_ref[0])
    tc = hi.shape[0]
    beta_wide = _dot(hi, ex_ref[:, :2 * WIDTH])
    gc_wide = _dot(jnp.concatenate([hi, lo], axis=0), ex_ref[:, 2 * WIDTH:])
    gc_wide = gc_wide[:tc] + gc_wide[tc:]
    wide = lambda a, c, grp, d: a[c * CHUNK:(c + 1) * CHUNK, d * WIDTH + grp * GW:d * WIDTH + (grp + 1) * GW]

    both = {t: _dot_nt(jnp.concatenate([tile(k_ref, *t), tile(q_ref, *t)], axis=0).astype(BF16), bd(tile(k_ref, *t)))
            for t in tiles}

    row = lax.broadcasted_iota(jnp.int32, dmat.shape, 0)
    col = row - dmat
    same16, same32 = (row // 16) == (col // 16), (row // 32) == (col // 32)
    qk, inv, diag, off32, off64 = {}, {}, {}, {}, {}
    for c, grp, d in units:
        beta, gc = wide(beta_wide, c, grp, d), wide(gc_wide, c, grp, d)
        incl, strict = (dmat <= 0, dmat < 0) if d else (dmat >= 0, dmat > 0)
        g_row = jnp.sum(gc * eye, axis=0, keepdims=True)
        decay = jnp.exp2(jnp.where(incl, gc - g_row, NEG_INF))
        kk_qk = both[c, grp]
        low = jnp.where(strict, kk_qk[:CHUNK] * beta * decay, 0.0)
        qk[c, grp, d] = kk_qk[CHUNK:] * decay
        diag[c, grp, d] = jnp.where(same16, low, 0.0)
        off32[c, grp, d] = jnp.where(same32 & ~same16, low, 0.0)
        off64[c, grp, d] = jnp.where(same32, 0.0, low)
        inv[c, grp, d] = eye - diag[c, grp, d]

    power = {un: _bdot(diag[un], bd(diag[un])) for un in units}
    for _ in range(2):
        res = {un: _bdot(jnp.concatenate([power[un], inv[un]], axis=0), bd(power[un])) for un in units}
        power = {un: res[un][:CHUNK] for un in units}
        inv = {un: inv[un] + res[un][CHUNK:] for un in units}
    inv = {un: inv[un] + _bdot(inv[un], bd(power[un])) for un in units}
    for off in (off32, off64):
        half = {un: _bdot(inv[un], bd(off[un])) for un in units}
        inv = {un: inv[un] - _bdot(half[un], bd(inv[un])) for un in units}

    for c, grp, d in units:
        un = (c, grp, d)
        q, k, v = tile(q_ref, c, grp), tile(k_ref, c, grp), tile(v_ref, c, grp)
        beta, gc = wide(beta_wide, c, grp, d), wide(gc_wide, c, grp, d)
        last = 0 if d else CHUNK - 1
        eg = jnp.exp2(gc)
        g_last = gc[last:last + 1, :]
        u = _bdot(inv[un], bd(v * beta))
        wk = _bdot(inv[un], bd(k * beta * eg))
        k_dec = k * jnp.exp2(g_last - gc)
        u_ref, wk_ref, qk_ref, qd_ref, kdt_ref, gt_ref = out_refs[d * n_out:(d + 1) * n_out]
        rows, sl = slice(c * CHUNK, (c + 1) * CHUNK), slice(grp * GW, (grp + 1) * GW)
        u_ref[0, rows, sl] = u.astype(BF16)
        wk_ref[0, rows, sl] = wk.astype(BF16)
        qk_ref[0, rows, sl] = qk[un].astype(BF16)
        qd_ref[0, rows, sl] = (q * eg).astype(BF16)
        k_dec_t = k_dec.T.astype(BF16)
        for h in range(GROUP):
            kdt_ref[0, rows, grp * GW + h * HEAD_DIM:grp * GW + (h + 1) * HEAD_DIM] = (
                k_dec_t[h * HEAD_DIM:(h + 1) * HEAD_DIM, :])
        gt_ref[0, c, :, sl] = jnp.exp2(g_last)


def _chunk_terms(q, k, v, bgc, expand, dmat, bdmask, chunks):
    b, t, w = q.shape
    n = t // CHUNK
    tc = chunks * CHUNK
    tok = pl.BlockSpec((1, tc, w), lambda bi, i: (bi, i, 0))
    full = lambda a: pl.BlockSpec(a.shape, lambda bi, i: (0, 0))
    per_dir_specs = [tok] * 5 + [pl.BlockSpec((1, chunks, 1, w), lambda bi, i: (bi, i, 0, 0))]
    act = jax.ShapeDtypeStruct((b, t, w), BF16)
    per_dir_shapes = [act] * 5 + [jax.ShapeDtypeStruct((b, n, 1, w), F32)]
    return pl.pallas_call(
        functools.partial(_chunk_body, chunks=chunks),
        grid=(b, t // tc),
        in_specs=[tok] * 3 + [pl.BlockSpec((1, tc, 128), lambda bi, i: (bi, i, 0)), full(expand), full(dmat),
                  full(bdmask)],
        out_specs=per_dir_specs * 2,
        out_shape=per_dir_shapes * 2,
        compiler_params=pltpu.CompilerParams(dimension_semantics=("parallel", "parallel"),
                                             vmem_limit_bytes=VMEM_LIMIT),
        name="chunk",
    )(q, k, v, bgc, expand, dmat, bdmask)


SEQ_CHUNKS = 4


def _seq_body(*refs, batch):
    fwd, bwd = refs[0:6], refs[6:12]
    s0_ref, mask_ref, of_ref, ob_ref, s_ref = refs[12:]

    @pl.when(pl.program_id(0) == 0)
    def _():
        s_ref[...] = s0_ref[...]

    bdmask = mask_ref[...]
    dirs = ((fwd, of_ref), (bwd, ob_ref))
    units = [(b, d, grp) for b in range(batch) for d in range(2) for grp in range(HEADS // GROUP)]
    lanes = lambda grp: slice(grp * GW, (grp + 1) * GW)

    for sub in range(SEQ_CHUNKS):
        chunk_of = (sub, SEQ_CHUNKS - 1 - sub)
        rows = [slice(c * CHUNK, (c + 1) * CHUNK) for c in chunk_of]
        proj = {}
        for b, d, grp in units:
            _, wk_ref, _, qd_ref, _, _ = dirs[d][0]
            lhs = jnp.concatenate([wk_ref[b, rows[d], lanes(grp)], qd_ref[b, rows[d], lanes(grp)]], axis=0)
            proj[b, d, grp] = _dot(lhs, _block_diag(s_ref[b, d, grp], bdmask))
        for b, d, grp in units:
            (u_ref, _, qk_ref, _, kdt_ref, gt_ref), o_ref = dirs[d]
            v_new = u_ref[b, rows[d], lanes(grp)].astype(F32) - proj[b, d, grp][:CHUNK]
            lhs = jnp.concatenate([qk_ref[b, rows[d], lanes(grp)], kdt_ref[b, rows[d], lanes(grp)]], axis=0)
            res = _dot(lhs, _block_diag(v_new, bdmask))
            o_ref[b, rows[d], lanes(grp)] = (proj[b, d, grp][CHUNK:] + res[:CHUNK]).astype(o_ref.dtype)
            s_ref[b, d, grp] = s_ref[b, d, grp] * gt_ref[b, chunk_of[d], :, lanes(grp)] + res[CHUNK:]


def _delta_scan(terms, s0, bdmask):
    b, t, w = terms[0].shape
    steps = t // (SEQ_CHUNKS * CHUNK)

    def specs(idx):
        return ([pl.BlockSpec((b, SEQ_CHUNKS * CHUNK, w), lambda i: (0, idx(i), 0))] * 5
                + [pl.BlockSpec((b, SEQ_CHUNKS, 1, w), lambda i: (0, idx(i), 0, 0))])

    forward, backward = (lambda i: i), (lambda i: steps - 1 - i)
    st = pl.BlockSpec(s0.shape, lambda i: (0,) * s0.ndim)
    o = jax.ShapeDtypeStruct((b, t, w), BF16)
    return pl.pallas_call(
        functools.partial(_seq_body, batch=b),
        grid=(steps,),
        in_specs=specs(forward) + specs(backward) + [st, pl.BlockSpec(bdmask.shape, lambda i: (0, 0))],
        out_specs=[specs(forward)[0], specs(backward)[0], st],
        out_shape=[o, o, jax.ShapeDtypeStruct(s0.shape, F32)],
        compiler_params=pltpu.CompilerParams(dimension_semantics=("arbitrary",), vmem_limit_bytes=VMEM_LIMIT),
        name="seq",
    )(*terms, s0, bdmask)


def _merge_body(x_ref, ona_ref, gzna_ref, of_ref, ob_ref, gzdn_ref, sgna_ref, sgdn_ref, gate_ref, dnw_ref, e_ref,
                wna_ref, wdn_ref, wout_ref, o_ref):
    a = (ona_ref[0].astype(F32) * gzna_ref[0].astype(F32)).astype(BF16)
    u_na = _dot(a, wna_ref[...])
    od = of_ref[0].astype(F32) + ob_ref[0].astype(F32)
    odn = od * lax.rsqrt(_seg_sum(od * od, e_ref) * (1.0 / HEAD_DIM) + EPS) * dnw_ref[...]
    u_dn = _dot((odn * gzdn_ref[0].astype(F32)).astype(BF16), wdn_ref[...])
    y = sgna_ref[0].astype(F32) * u_na + sgdn_ref[0].astype(F32) * u_dn
    o_ref[0] = x_ref[0] + gate_ref[0] * _dot(y.astype(BF16), wout_ref[...])


def _merge(x, o_na, gz_na, o_f, o_b, gz_dn, sg_na, sg_dn, gate, dnw, e64, w_o_na, w_o_dn, w_out, tm):
    b, t, d = x.shape
    w = WIDTH
    tok = lambda width: pl.BlockSpec((1, tm, width), lambda bi, i: (bi, i, 0))
    full = lambda a: pl.BlockSpec(a.shape, lambda bi, i: (0, 0))
    return pl.pallas_call(
        _merge_body,
        grid=(b, t // tm),
        in_specs=[tok(d), tok(w), tok(w), tok(w), tok(w), tok(w), tok(d), tok(d),
                  pl.BlockSpec((1, 1, d), lambda bi, i: (bi, 0, 0)), full(dnw), full(e64),
                  full(w_o_na), full(w_o_dn), full(w_out)],
        out_specs=tok(d),
        out_shape=jax.ShapeDtypeStruct((b, t, d), F32),
        compiler_params=pltpu.CompilerParams(dimension_semantics=("parallel", "parallel"),
                                             vmem_limit_bytes=VMEM_LIMIT),
        name="merge",
    )(x, o_na, gz_na, o_f, o_b, gz_dn, sg_na, sg_dn, gate, dnw, e64, w_o_na, w_o_dn, w_out)


def _constants(tc):
    seg = np.arange(GW) // HEAD_DIM
    e64 = (seg[:, None] == seg[None, :]).astype(np.float32)
    tok = np.arange(tc)
    same_chunk = (tok[:, None] // CHUNK) == (tok[None, :] // CHUNK)
    tril = (same_chunk & (tok[None, :] <= tok[:, None])).astype(np.float32)
    triu = (same_chunk & (tok[None, :] >= tok[:, None])).astype(np.float32)
    expand = np.zeros((128, 4 * WIDTH), np.float32)
    for s in range(4):
        for h in range(HEADS):
            expand[s * HEADS + h, s * WIDTH + h * HEAD_DIM:s * WIDTH + (h + 1) * HEAD_DIM] = 1.0
    lane = np.arange(GW)
    dmat = (np.arange(CHUNK)[:, None] - (lane % HEAD_DIM)[None, :]).astype(np.int32)
    bdmask = ((lane[:, None] // HEAD_DIM) == (lane[None, :] // HEAD_DIM)).astype(np.float32)
    return (jnp.asarray(e64, BF16), jnp.asarray(tril, BF16), jnp.asarray(triu, BF16), jnp.asarray(expand, BF16),
            jnp.asarray(dmat), jnp.asarray(bdmask, BF16))


def _rope_tables(t):
    half = HEAD_DIM // 4
    rows = t // GRID_W
    d = jnp.arange(2 * HEAD_DIM) % HEAD_DIM
    freqs = ROPE_BASE ** (-(d % half).astype(F32) / half)
    by_row = (d < HEAD_DIM // 2)[None, None, :]
    ang_r = jnp.arange(rows).astype(F32)[:, None] * freqs[None, :]
    ang_c = jnp.arange(GRID_W).astype(F32)[:, None] * freqs[None, :]
    grid = lambda f: jnp.where(by_row, f(ang_r)[:, None, :], f(ang_c)[None, :, :]).reshape(t, 2 * HEAD_DIM)
    cos, sin = grid(jnp.cos), grid(jnp.sin)
    lower = ((d % (2 * half)) < half)[None, :]
    return cos, jnp.where(lower, -sin, 0.0), jnp.where(lower, 0.0, sin)


def _pad_lanes(a, offset):
    return jnp.zeros((1, 128), F32).at[0, offset:offset + a.size].set(a.reshape(-1))


def _layer(x, ctx, c, c_ctx, mod_w, mod_b, norm_w, w_in, conv_w, na_q_norm, na_k_norm, na_rpb, dn_A_log, dn_dt_bias,
           dn_norm_w, w_o_na, w_o_dn, w_out):
    b, t, d = x.shape
    w = WIDTH
    tm = 256
    e64, tril, triu, expand, dmat, bdmask = _constants(CUM_ROWS)

    cc = jnp.zeros((8, d), F32).at[:b].set(c).at[b].set(c_ctx)
    mod = _modulation(cc, mod_w, mod_b)
    shift, scale, gate = mod[:, :d], mod[:, d:2 * d], mod[:, 2 * d:]
    rows_x = lambda a: a[:b, None, :]
    rows_c = lambda a: jnp.broadcast_to(a[b][None, None, :], (b, 1, d))

    n_ba = 4 * HEADS
    ba0 = 8 * w
    w_main = jnp.concatenate([w_in[:, :ba0], w_in[:, ba0 + n_ba:]], axis=1).astype(BF16)
    w_ba = jnp.zeros((d, 128), F32).at[:, :n_ba].set(w_in[:, ba0:ba0 + n_ba]).astype(BF16)
    nw = norm_w.reshape(1, d)
    qw = jnp.tile(na_q_norm, HEADS).reshape(1, w)
    kw = jnp.tile(na_k_norm, HEADS).reshape(1, w)
    alog = _pad_lanes(dn_A_log, 2 * HEADS)
    dtb = _pad_lanes(dn_dt_bias, 2 * HEADS)
    project = functools.partial(_project, nw=nw, w_main=w_main, w_ba=w_ba, qw=qw, kw=kw, e64=e64, alog=alog,
                                dtb=dtb, conv_w=conv_w, tril=tril, triu=triu)
    q_na, k_na, v_na, gz_na, gz_dn, sg_na, sg_dn, *dn = project(x, rows_x(scale), rows_x(shift),
                                                                 tables=_rope_tables(t), tm=2 * tm)
    _, k_c, v_c, _, _, _, _, *dn_c = project(ctx, rows_c(scale), rows_c(shift), tables=None, tm=tm)

    bias = _bias_table(na_rpb, t // GRID_W)
    o_na = _neighbourhood_attention(q_na, k_na, v_na, k_c, v_c, bias)

    s_zero = jnp.zeros((b, 2, HEADS // GROUP, HEAD_DIM, GW), F32)
    _, _, s_ctx = _delta_scan(_chunk_terms(*dn_c, expand, dmat, bdmask, chunks=4), s_zero, bdmask)
    o_f, o_b, _ = _delta_scan(_chunk_terms(*dn, expand, dmat, bdmask, chunks=4), s_ctx, bdmask)

    dnw = jnp.tile(dn_norm_w, HEADS).reshape(1, w)
    return _merge(x, o_na, gz_na, o_f, o_b, gz_dn, sg_na, sg_dn, rows_x(gate), dnw, e64,
                  w_o_na.astype(BF16), w_o_dn.astype(BF16), w_out.astype(BF16), 2 * tm)


def kernel(x, c, ctx, c_ctx, mod_w, mod_b, norm_w, w_in, conv_w, na_q_norm, na_k_norm, na_rpb, dn_A_log, dn_dt_bias,
           dn_norm_w, w_o_na, w_o_dn, w_out):
    depth = mod_w.shape[0]
    assert depth == 1, "context-stream update between layers is not implemented"
    return _layer(x, ctx, c, c_ctx, mod_w[0], mod_b[0], norm_w[0], w_in[0], conv_w[0], na_q_norm[0], na_k_norm[0],
                  na_rpb[0], dn_A_log[0], dn_dt_bias[0], dn_norm_w[0], w_o_na[0], w_o_dn[0], w_out[0])
```

```python
import functools

import jax
import jax.numpy as jnp
import numpy as np
from jax import lax
from jax.experimental import pallas as pl
from jax.experimental.pallas import tpu as pltpu

F32 = jnp.float32
BF16 = jnp.bfloat16
EPS = 1e-6
GRID_W = 64
HEADS = 8
HEAD_DIM = 64
WIDTH = HEADS * HEAD_DIM
NA_KH = 8
NA_KW = 16
CHUNK = 64
ROPE_BASE = 10000.0
GROUP = 4
GW = GROUP * HEAD_DIM
NEG_INF = float("-inf")
LOG2E = 1.4426950408889634
HIGHEST = lax.Precision.HIGHEST
VMEM_LIMIT = 56 * 1024 * 1024


def _dot(a, b, **kw):
    return jnp.dot(a, b, preferred_element_type=F32, **kw)


def _dot_nt(a, b):
    return lax.dot_general(a, b, (((1,), (1,)), ((), ())), preferred_element_type=F32)


def _seg_sum(x, e_ref):
    xb = x.astype(BF16)
    e = e_ref[...]
    return jnp.concatenate([_dot(xb[:, g * GW:(g + 1) * GW], e) for g in range(x.shape[1] // GW)], axis=1)


def _silu(x):
    return x * jax.nn.sigmoid(x)


def _mod_body(c_ref, w_ref, b_ref, o_ref):
    o_ref[...] = _dot(_silu(c_ref[...]), w_ref[...], precision=HIGHEST) + b_ref[...]


def _modulation(cc, mod_w, mod_b):
    rows, d = cc.shape
    n = mod_w.shape[1]
    tn = 512
    return pl.pallas_call(
        _mod_body,
        grid=(n // tn,),
        in_specs=[pl.BlockSpec((rows, d), lambda j: (0, 0)),
                  pl.BlockSpec((d, tn), lambda j: (0, j)),
                  pl.BlockSpec((1, tn), lambda j: (0, j))],
        out_specs=pl.BlockSpec((rows, tn), lambda j: (0, j)),
        out_shape=jax.ShapeDtypeStruct((rows, n), F32),
        compiler_params=pltpu.CompilerParams(dimension_semantics=("parallel",)),
        name="mod",
    )(cc, mod_w, mod_b.reshape(1, n))


CUM_ROWS = 256


def _split2(x):
    hi = x.astype(BF16)
    return hi, (x - hi.astype(F32)).astype(BF16)


def _proj_body(*refs, rope):
    (x_ref, xp_ref, xn_ref, scale_ref, shift_ref, nw_ref, w_ref, wba_ref, qw_ref, kw_ref, e_ref, alog_ref, dtb_ref,
     cw_ref, tril_ref, triu_ref) = refs[:16]
    tabs = refs[16:19] if rope else ()
    q_ref, kt_ref, v_ref, gzna_ref, gzdn_ref, sgna_ref, sgdn_ref, dq_ref, dk_ref, dv_ref, bgc_ref = refs[-11:]
    i = pl.program_id(1)
    last = pl.num_programs(1) - 1

    def modulated(xv):
        xn = xv * lax.rsqrt(jnp.mean(xv * xv, axis=-1, keepdims=True) + EPS)
        return ((xn * nw_ref[...]) * (1.0 + scale_ref[0]) + shift_ref[0]).astype(BF16)

    hb = modulated(x_ref[0])
    tm = hb.shape[0]
    halo = modulated(jnp.concatenate([xp_ref[0, 0], xn_ref[0, 0]], axis=0))

    def mm(lo, hi):
        return _dot(hb, w_ref[:, lo:hi])

    def head_rms(a, w_row):
        return a * lax.rsqrt(_seg_sum(a * a, e_ref) * (1.0 / HEAD_DIM) + EPS) * w_row

    w = WIDTH
    q_ref[0] = (head_rms(mm(0, w), qw_ref[...]) * (HEAD_DIM ** -0.5 * LOG2E)).astype(BF16)
    k_t = head_rms(mm(w, 2 * w), kw_ref[...]).T.astype(BF16)
    for j in range(kt_ref.shape[1]):
        kt_ref[0, j] = k_t[:, j * 128:(j + 1) * 128]
    v_ref[0] = mm(2 * w, 3 * w).astype(BF16)
    gzna_ref[0] = _silu(mm(3 * w, 4 * w)).astype(BF16)
    gzdn_ref[0] = _silu(mm(7 * w, 8 * w)).astype(BF16)
    sgna_ref[0] = jax.nn.sigmoid(mm(8 * w, 10 * w)).astype(BF16)
    sgdn_ref[0] = jax.nn.sigmoid(mm(10 * w, 12 * w)).astype(BF16)

    raw = _dot(jnp.concatenate([hb, halo], axis=0), w_ref[:, 4 * w:7 * w])
    xc = raw[:tm]
    before = jnp.where(i > 0, raw[tm + 7:tm + 8], 0.0)
    after = jnp.where(i < last, raw[tm + 8:tm + 9], 0.0)
    rowid = lax.broadcasted_iota(jnp.int32, xc.shape, 0)
    prev = jnp.where(rowid == 0, before, pltpu.roll(xc, 1, axis=0))
    nxt = jnp.where(rowid == tm - 1, after, pltpu.roll(xc, tm - 1, axis=0))
    y = _silu(prev * cw_ref[0:1, :] + xc * cw_ref[1:2, :] + nxt * cw_ref[2:3, :])
    half = HEAD_DIM // 4

    def norm_rope(a):
        a = a * lax.rsqrt(_seg_sum(a * a, e_ref) + EPS)
        if not rope:
            return a
        heads = lambda ref: jnp.concatenate([ref[...]] * (w // ref.shape[1]), axis=1)
        cos_ref, sina_ref, sinb_ref = tabs
        return (a * heads(cos_ref) + pltpu.roll(a, w - half, axis=1) * heads(sina_ref)
                + pltpu.roll(a, half, axis=1) * heads(sinb_ref))

    dq_ref[0] = norm_rope(y[:, :w]) * HEAD_DIM ** -0.5
    dk_ref[0] = norm_rope(y[:, w:2 * w])
    dv_ref[0] = y[:, 2 * w:]

    ba = _dot(hb, wba_ref[...])
    lane = lax.broadcasted_iota(jnp.int32, (CUM_ROWS, ba.shape[1]), 1)
    a = ba + dtb_ref[...]
    softplus = jnp.maximum(a, 0.0) + jnp.log1p(jnp.exp(-jnp.abs(a)))
    g = -jnp.exp(alog_ref[...]) * softplus
    beta = jax.nn.sigmoid(ba)
    n = ba.shape[1]
    for s in range(tm // CUM_ROWS):
        rows = slice(s * CUM_ROWS, (s + 1) * CUM_ROWS)
        parts = jnp.concatenate(_split2(g[rows]), axis=1)
        cum_f = _dot(tril_ref[...], parts)
        cum_b = _dot(triu_ref[...], parts)
        cum_f = (cum_f[:, :n] + cum_f[:, n:]) * LOG2E
        cum_b = (cum_b[:, :n] + cum_b[:, n:]) * LOG2E
        bgc_ref[0, rows] = jnp.where(lane < 2 * HEADS, beta[rows],
                                     jnp.where(lane < 3 * HEADS, cum_f, jnp.where(lane < 4 * HEADS, cum_b, 0.0)))


def _project(x, scale, shift, nw, w_main, w_ba, qw, kw, e64, alog, dtb, conv_w, tril, triu, tables, tm):
    b, t, d = x.shape
    w = WIDTH
    per8 = tm // 8
    nblk8 = t // 8
    x8 = x.reshape(b, nblk8, 8, d)
    tok = lambda width: pl.BlockSpec((1, tm, width), lambda bi, i: (bi, i, 0))
    row = lambda width: pl.BlockSpec((1, width), lambda bi, i: (0, 0))
    per_batch = pl.BlockSpec((1, 1, d), lambda bi, i: (bi, 0, 0))
    full = lambda a: pl.BlockSpec(a.shape, lambda bi, i: (0, 0))
    tab = pl.BlockSpec((tm, 2 * HEAD_DIM), lambda bi, i: (i, 0))
    sds = lambda width, dt: jax.ShapeDtypeStruct((b, t, width), dt)
    tables = tuple(tables) if tables is not None else ()
    return pl.pallas_call(
        functools.partial(_proj_body, rope=bool(tables)),
        grid=(b, t // tm),
        in_specs=[tok(d),
                  pl.BlockSpec((1, 1, 8, d), lambda bi, i: (bi, jnp.maximum(i * per8 - 1, 0), 0, 0)),
                  pl.BlockSpec((1, 1, 8, d), lambda bi, i: (bi, jnp.minimum((i + 1) * per8, nblk8 - 1), 0, 0)),
                  per_batch, per_batch, row(d), full(w_main), full(w_ba), row(w), row(w), full(e64),
                  row(128), row(128), full(conv_w), full(tril), full(triu)] + [tab] * len(tables),
        out_specs=[tok(w), pl.BlockSpec((1, tm // 128, w, 128), lambda bi, i: (bi, i, 0, 0)), tok(w), tok(w), tok(w),
                   tok(2 * w), tok(2 * w), tok(w), tok(w), tok(w), tok(128)],
        out_shape=[sds(w, BF16), jax.ShapeDtypeStruct((b, t // 128, w, 128), BF16), sds(w, BF16), sds(w, BF16),
                   sds(w, BF16), sds(2 * w, BF16), sds(2 * w, BF16), sds(w, F32), sds(w, F32), sds(w, F32),
                   sds(128, F32)],
        compiler_params=pltpu.CompilerParams(dimension_semantics=("parallel", "parallel"),
                                             vmem_limit_bytes=VMEM_LIMIT),
        name="proj",
    )(x, x8, x8, scale, shift, nw, w_main, w_ba, qw, kw, e64, alog, dtb, conv_w, tril, triu, *tables)


NA_BAND = NA_KH + 2
NA_MAX_PAIRS_PER_STEP = 16
NA_BAND_OFFSETS = NA_KH // 2 + 1


def _band_plan(rows):
    y = np.arange(NA_BAND)
    dy = np.zeros((NA_BAND_OFFSETS, 2, NA_BAND), np.int32)
    valid = np.zeros((NA_BAND_OFFSETS, 2, NA_BAND), bool)
    for var, m in enumerate((0, 1, 2, rows // 2 - 2, rows // 2 - 1)):
        b0 = min(max(2 * m - NA_KH // 2, 0), rows - NA_BAND)
        assert (2 * m - b0) // 2 == var
        for e in range(2):
            r = 2 * m + e
            r0 = min(max(r - NA_KH // 2, 0), rows - NA_KH)
            valid[var, e] = (b0 + y >= r0) & (b0 + y < r0 + NA_KH)
            dy[var, e] = np.clip(b0 + y - r + NA_KH - 1, 0, 2 * NA_KH - 2)
    return dy, valid


def _bias_body(rpb_ref, o_ref, tiles_ref, *, dy, valid):
    p = pl.program_id(0)
    n_dy, n_dx = 2 * NA_KH - 1, 2 * NA_KW - 1
    cq = lax.broadcasted_iota(jnp.int32, (GRID_W, GRID_W), 0)
    ck = lax.broadcasted_iota(jnp.int32, (GRID_W, GRID_W), 1)
    c0 = jnp.clip(cq - NA_KW // 2, 0, GRID_W - NA_KW)
    col_in = (ck >= c0) & (ck < c0 + NA_KW)
    dx = jnp.clip(ck - cq, -(NA_KW - 1), NA_KW - 1) + (NA_KW - 1)
    for hh in range(2):
        for i in range(n_dy):
            acc = jnp.zeros((GRID_W, GRID_W), F32)
            for d in range(n_dx):
                acc = jnp.where(dx == d, rpb_ref[((2 * p + hh) * n_dy + i) * n_dx + d], acc)
            tiles_ref[hh, i] = jnp.where(col_in, acc * LOG2E, NEG_INF)
    outside = jnp.full((GRID_W, GRID_W), NEG_INF, F32)
    for var in range(NA_BAND_OFFSETS):
        for e in range(2):
            for hh in range(2):
                r = (2 * e + hh) * GRID_W
                for y in range(NA_BAND):
                    tile = tiles_ref[hh, int(dy[var, e, y])] if valid[var, e, y] else outside
                    o_ref[0, var, r:r + GRID_W, y * GRID_W:(y + 1) * GRID_W] = tile


def _bias_table(rpb, rows):
    dy, valid = _band_plan(rows)
    shape = (HEADS // 2, NA_BAND_OFFSETS, 4 * GRID_W, NA_BAND * GRID_W)
    return pl.pallas_call(
        functools.partial(_bias_body, dy=dy, valid=valid),
        grid=(HEADS // 2,),
        in_specs=[pl.BlockSpec(memory_space=pltpu.SMEM)],
        out_specs=pl.BlockSpec((1,) + shape[1:], lambda p: (p, 0, 0, 0)),
        out_shape=jax.ShapeDtypeStruct(shape, F32),
        scratch_shapes=[pltpu.VMEM((2, 2 * NA_KH - 1, GRID_W, GRID_W), F32)],
        compiler_params=pltpu.CompilerParams(dimension_semantics=("parallel",)),
        name="bias",
    )(rpb.reshape(-1))


def _na_body(q_ref, kt_ref, v_ref, kct_ref, vc_ref, bias_ref, o_ref, *, rows, n_pairs):
    j = pl.program_id(2)
    lane = lax.broadcasted_iota(jnp.int32, (GRID_W, 2 * HEAD_DIM), 1)
    first = lane < HEAD_DIM
    kct = jnp.concatenate([kct_ref[0, i] for i in range(kct_ref.shape[1])], axis=1)
    with_ones = lambda v: jnp.concatenate([v, jnp.ones_like(v)], axis=1)
    vc = with_ones(vc_ref[0])
    tiles_per_band = NA_BAND * GRID_W // 128
    pairs = range(n_pairs)

    lhs, s_win, s_ctx, start = {}, {}, {}, {}
    for mm in pairs:
        m = j * n_pairs + mm
        b0 = jnp.clip(2 * m - NA_KH // 2, 0, rows - NA_BAND)
        variant = (2 * m - b0) // 2
        start[mm] = b0 // 2
        parts = []
        for e in range(2):
            qr = q_ref[0, (2 * mm + e) * GRID_W:(2 * mm + e + 1) * GRID_W, :]
            zero = jnp.zeros_like(qr)
            parts += [jnp.where(first, qr, zero), jnp.where(first, zero, qr)]
        lhs[mm] = jnp.concatenate(parts, axis=0)
        kt = jnp.concatenate([kt_ref[0, start[mm] + i] for i in range(tiles_per_band)], axis=1)
        s_win[mm] = _dot(lhs[mm], kt) + bias_ref[0, variant]
        s_ctx[mm] = _dot(lhs[mm], kct)
    p_win, p_ctx = {}, {}
    for mm in pairs:
        peak = jnp.maximum(jnp.max(s_win[mm], axis=-1, keepdims=True), jnp.max(s_ctx[mm], axis=-1, keepdims=True))
        p_win[mm] = jnp.exp2(s_win[mm] - peak).astype(BF16)
        p_ctx[mm] = jnp.exp2(s_ctx[mm] - peak).astype(BF16)
    for mm in pairs:
        vb = with_ones(v_ref[0, pl.ds(pl.multiple_of(start[mm] * 128, 128), NA_BAND * GRID_W), :])
        o = _dot(p_win[mm], vb) + _dot(p_ctx[mm], vc)
        o = o[:, :2 * HEAD_DIM] / o[:, 2 * HEAD_DIM:]
        for e in range(2):
            top = o[2 * e * GRID_W:(2 * e + 1) * GRID_W]
            bottom = o[(2 * e + 1) * GRID_W:(2 * e + 2) * GRID_W]
            o_ref[0, (2 * mm + e) * GRID_W:(2 * mm + e + 1) * GRID_W, :] = jnp.where(first, top, bottom).astype(
                o_ref.dtype)


def _neighbourhood_attention(q, kt, v, kct, vc, bias):
    b, t, _ = q.shape
    ctx_len = vc.shape[1]
    rows = t // GRID_W
    n_pairs = NA_MAX_PAIRS_PER_STEP
    while rows % (2 * n_pairs):
        n_pairs //= 2
    rows_per_step = 2 * n_pairs
    assert rows >= NA_BAND + 2 and rows % 2 == 0
    tq = rows_per_step * GRID_W
    pair_w = 2 * HEAD_DIM
    return pl.pallas_call(
        functools.partial(_na_body, rows=rows, n_pairs=n_pairs),
        grid=(b, HEADS // 2, t // tq),
        in_specs=[pl.BlockSpec((1, tq, pair_w), lambda bi, p, j: (bi, j, p)),
                  pl.BlockSpec((1, t // 128, pair_w, 128), lambda bi, p, j: (bi, 0, p, 0)),
                  pl.BlockSpec((1, t, pair_w), lambda bi, p, j: (bi, 0, p)),
                  pl.BlockSpec((1, ctx_len // 128, pair_w, 128), lambda bi, p, j: (bi, 0, p, 0)),
                  pl.BlockSpec((1, ctx_len, pair_w), lambda bi, p, j: (bi, 0, p)),
                  pl.BlockSpec((1,) + bias.shape[1:], lambda bi, p, j: (p, 0, 0, 0))],
        out_specs=pl.BlockSpec((1, tq, pair_w), lambda bi, p, j: (bi, j, p)),
        out_shape=jax.ShapeDtypeStruct((b, t, WIDTH), BF16),
        compiler_params=pltpu.CompilerParams(dimension_semantics=("parallel", "parallel", "arbitrary"),
                                             vmem_limit_bytes=VMEM_LIMIT),
        name="na",
    )(q, kt, v, kct, vc, bias)


def _block_diag(x, mask):
    return jnp.concatenate([x.astype(BF16)] * GROUP, axis=0) * mask


def _bdot(a, b):
    return _dot(a.astype(BF16), b.astype(BF16))


def _chunk_body(q_ref, k_ref, v_ref, bgc_ref, ex_ref, dmat_ref, mask_ref, *out_refs, chunks):
    dmat = dmat_ref[...]
    bdmask = mask_ref[...]
    eye = (dmat == 0).astype(F32)
    n_out = len(out_refs) // 2
    bd = lambda x: _block_diag(x, bdmask)
    tile = lambda ref, c, grp: ref[0, c * CHUNK:(c + 1) * CHUNK, grp * GW:(grp + 1) * GW]
    tiles = [(c, grp) for c in range(chunks) for grp in range(HEADS // GROUP)]
    units = [(c, grp, d) for c, grp in tiles for d in range(2)]

    hi, lo = _split2(bgc_ref[0])
    tc = hi.shape[0]
    beta_wide = _dot(hi, ex_ref[:, :2 * WIDTH])
    gc_wide = _dot(jnp.concatenate([hi, lo], axis=0), ex_ref[:, 2 * WIDTH:])
    gc_wide = gc_wide[:tc] + gc_wide[tc:]
    wide = lambda a, c, grp, d: a[c * CHUNK:(c + 1) * CHUNK, d * WIDTH + grp * GW:d * WIDTH + (grp + 1) * GW]

    both = {t: _dot_nt(jnp.concatenate([tile(k_ref, *t), tile(q_ref, *t)], axis=0).astype(BF16), bd(tile(k_ref, *t)))
            for t in tiles}

    row = lax.broadcasted_iota(jnp.int32, dmat.shape, 0)
    col = row - dmat
    same16, same32 = (row // 16) == (col // 16), (row // 32) == (col // 32)
    qk, inv, diag, off32, off64 = {}, {}, {}, {}, {}
    for c, grp, d in units:
        beta, gc = wide(beta_wide, c, grp, d), wide(gc_wide, c, grp, d)
        incl, strict = (dmat <= 0, dmat < 0) if d else (dmat >= 0, dmat > 0)
        g_row = jnp.sum(gc * eye, axis=0, keepdims=True)
        decay = jnp.exp2(jnp.where(incl, gc - g_row, NEG_INF))
        kk_qk = both[c, grp]
        low = jnp.where(strict, kk_qk[:CHUNK] * beta * decay, 0.0)
        qk[c, grp, d] = kk_qk[CHUNK:] * decay
        diag[c, grp, d] = jnp.where(same16, low, 0.0)
        off32[c, grp, d] = jnp.where(same32 & ~same16, low, 0.0)
        off64[c, grp, d] = jnp.where(same32, 0.0, low)
        inv[c, grp, d] = eye - diag[c, grp, d]

    power = {un: _bdot(diag[un], bd(diag[un])) for un in units}
    for _ in range(2):
        res = {un: _bdot(jnp.concatenate([power[un], inv[un]], axis=0), bd(power[un])) for un in units}
        power = {un: res[un][:CHUNK] for un in units}
        inv = {un: inv[un] + res[un][CHUNK:] for un in units}
    inv = {un: inv[un] + _bdot(inv[un], bd(power[un])) for un in units}
    for off in (off32, off64):
        half = {un: _bdot(inv[un], bd(off[un])) for un in units}
        inv = {un: inv[un] - _bdot(half[un], bd(inv[un])) for un in units}

    for c, grp, d in units:
        un = (c, grp, d)
        q, k, v = tile(q_ref, c, grp), tile(k_ref, c, grp), tile(v_ref, c, grp)
        beta, gc = wide(beta_wide, c, grp, d), wide(gc_wide, c, grp, d)
        last = 0 if d else CHUNK - 1
        eg = jnp.exp2(gc)
        g_last = gc[last:last + 1, :]
        u = _bdot(inv[un], bd(v * beta))
        wk = _bdot(inv[un], bd(k * beta * eg))
        k_dec = k * jnp.exp2(g_last - gc)
        u_ref, wk_ref, qk_ref, qd_ref, kdt_ref, gt_ref = out_refs[d * n_out:(d + 1) * n_out]
        rows, sl = slice(c * CHUNK, (c + 1) * CHUNK), slice(grp * GW, (grp + 1) * GW)
        u_ref[0, rows, sl] = u.astype(BF16)
        wk_ref[0, rows, sl] = wk.astype(BF16)
        qk_ref[0, rows, sl] = qk[un].astype(BF16)
        qd_ref[0, rows, sl] = (q * eg).astype(BF16)
        k_dec_t = k_dec.T.astype(BF16)
        for h in range(GROUP):
            kdt_ref[0, rows, grp * GW + h * HEAD_DIM:grp * GW + (h + 1) * HEAD_DIM] = (
                k_dec_t[h * HEAD_DIM:(h + 1) * HEAD_DIM, :])
        gt_ref[0, c, :, sl] = jnp.exp2(g_last)


def _chunk_terms(q, k, v, bgc, expand, dmat, bdmask, chunks):
    b, t, w = q.shape
    n = t // CHUNK
    tc = chunks * CHUNK
    tok = pl.BlockSpec((1, tc, w), lambda bi, i: (bi, i, 0))
    full = lambda a: pl.BlockSpec(a.shape, lambda bi, i: (0, 0))
    per_dir_specs = [tok] * 5 + [pl.BlockSpec((1, chunks, 1, w), lambda bi, i: (bi, i, 0, 0))]
    act = jax.ShapeDtypeStruct((b, t, w), BF16)
    per_dir_shapes = [act] * 5 + [jax.ShapeDtypeStruct((b, n, 1, w), F32)]
    return pl.pallas_call(
        functools.partial(_chunk_body, chunks=chunks),
        grid=(b, t // tc),
        in_specs=[tok] * 3 + [pl.BlockSpec((1, tc, 128), lambda bi, i: (bi, i, 0)), full(expand), full(dmat),
                  full(bdmask)],
        out_specs=per_dir_specs * 2,
        out_shape=per_dir_shapes * 2,
        compiler_params=pltpu.CompilerParams(dimension_semantics=("parallel", "parallel"),
                                             vmem_limit_bytes=VMEM_LIMIT),
        name="chunk",
    )(q, k, v, bgc, expand, dmat, bdmask)


SEQ_CHUNKS = 4


def _seq_body(*refs, batch):
    fwd, bwd = refs[0:6], refs[6:12]
    s0_ref, mask_ref, of_ref, ob_ref, s_ref = refs[12:]

    @pl.when(pl.program_id(0) == 0)
    def _():
        s_ref[...] = s0_ref[...]

    bdmask = mask_ref[...]
    dirs = ((fwd, of_ref), (bwd, ob_ref))
    units = [(b, d, grp) for b in range(batch) for d in range(2) for grp in range(HEADS // GROUP)]
    lanes = lambda grp: slice(grp * GW, (grp + 1) * GW)

    for sub in range(SEQ_CHUNKS):
        chunk_of = (sub, SEQ_CHUNKS - 1 - sub)
        rows = [slice(c * CHUNK, (c + 1) * CHUNK) for c in chunk_of]
        proj = {}
        for b, d, grp in units:
            _, wk_ref, _, qd_ref, _, _ = dirs[d][0]
            lhs = jnp.concatenate([wk_ref[b, rows[d], lanes(grp)], qd_ref[b, rows[d], lanes(grp)]], axis=0)
            proj[b, d, grp] = _dot(lhs, _block_diag(s_ref[b, d, grp], bdmask))
        for b, d, grp in units:
            (u_ref, _, qk_ref, _, kdt_ref, gt_ref), o_ref = dirs[d]
            v_new = u_ref[b, rows[d], lanes(grp)].astype(F32) - proj[b, d, grp][:CHUNK]
            lhs = jnp.concatenate([qk_ref[b, rows[d], lanes(grp)], kdt_ref[b, rows[d], lanes(grp)]], axis=0)
            res = _dot(lhs, _block_diag(v_new, bdmask))
            o_ref[b, rows[d], lanes(grp)] = (proj[b, d, grp][CHUNK:] + res[:CHUNK]).astype(o_ref.dtype)
            s_ref[b, d, grp] = s_ref[b, d, grp] * gt_ref[b, chunk_of[d], :, lanes(grp)] + res[CHUNK:]


def _delta_scan(terms, s0, bdmask):
    b, t, w = terms[0].shape
    steps = t // (SEQ_CHUNKS * CHUNK)

    def specs(idx):
        return ([pl.BlockSpec((b, SEQ_CHUNKS * CHUNK, w), lambda i: (0, idx(i), 0))] * 5
                + [pl.BlockSpec((b, SEQ_CHUNKS, 1, w), lambda i: (0, idx(i), 0, 0))])

    forward, backward = (lambda i: i), (lambda i: steps - 1 - i)
    st = pl.BlockSpec(s0.shape, lambda i: (0,) * s0.ndim)
    o = jax.ShapeDtypeStruct((b, t, w), BF16)
    return pl.pallas_call(
        functools.partial(_seq_body, batch=b),
        grid=(steps,),
        in_specs=specs(forward) + specs(backward) + [st, pl.BlockSpec(bdmask.shape, lambda i: (0, 0))],
        out_specs=[specs(forward)[0], specs(backward)[0], st],
        out_shape=[o, o, jax.ShapeDtypeStruct(s0.shape, F32)],
        compiler_params=pltpu.CompilerParams(dimension_semantics=("arbitrary",), vmem_limit_bytes=VMEM_LIMIT),
        name="seq",
    )(*terms, s0, bdmask)


def _merge_body(x_ref, ona_ref, gzna_ref, of_ref, ob_ref, gzdn_ref, sgna_ref, sgdn_ref, gate_ref, dnw_ref, e_ref,
                wna_ref, wdn_ref, wout_ref, o_ref):
    a = (ona_ref[0].astype(F32) * gzna_ref[0].astype(F32)).astype(BF16)
    u_na = _dot(a, wna_ref[...])
    od = of_ref[0].astype(F32) + ob_ref[0].astype(F32)
    odn = od * lax.rsqrt(_seg_sum(od * od, e_ref) * (1.0 / HEAD_DIM) + EPS) * dnw_ref[...]
    u_dn = _dot((odn * gzdn_ref[0].astype(F32)).astype(BF16), wdn_ref[...])
    y = sgna_ref[0].astype(F32) * u_na + sgdn_ref[0].astype(F32) * u_dn
    o_ref[0] = x_ref[0] + gate_ref[0] * _dot(y.astype(BF16), wout_ref[...])


def _merge(x, o_na, gz_na, o_f, o_b, gz_dn, sg_na, sg_dn, gate, dnw, e64, w_o_na, w_o_dn, w_out, tm):
    b, t, d = x.shape
    w = WIDTH
    tok = lambda width: pl.BlockSpec((1, tm, width), lambda bi, i: (bi, i, 0))
    full = lambda a: pl.BlockSpec(a.shape, lambda bi, i: (0, 0))
    return pl.pallas_call(
        _merge_body,
        grid=(b, t // tm),
        in_specs=[tok(d), tok(w), tok(w), tok(w), tok(w), tok(w), tok(d), tok(d),
                  pl.BlockSpec((1, 1, d), lambda bi, i: (bi, 0, 0)), full(dnw), full(e64),
                  full(w_o_na), full(w_o_dn), full(w_out)],
        out_specs=tok(d),
        out_shape=jax.ShapeDtypeStruct((b, t, d), F32),
        compiler_params=pltpu.CompilerParams(dimension_semantics=("parallel", "parallel"),
                                             vmem_limit_bytes=VMEM_LIMIT),
        name="merge",
    )(x, o_na, gz_na, o_f, o_b, gz_dn, sg_na, sg_dn, gate, dnw, e64, w_o_na, w_o_dn, w_out)


def _constants(tc):
    seg = np.arange(GW) // HEAD_DIM
    e64 = (seg[:, None] == seg[None, :]).astype(np.float32)
    tok = np.arange(tc)
    same_chunk = (tok[:, None] // CHUNK) == (tok[None, :] // CHUNK)
    tril = (same_chunk & (tok[None, :] <= tok[:, None])).astype(np.float32)
    triu = (same_chunk & (tok[None, :] >= tok[:, None])).astype(np.float32)
    expand = np.zeros((128, 4 * WIDTH), np.float32)
    for s in range(4):
        for h in range(HEADS):
            expand[s * HEADS + h, s * WIDTH + h * HEAD_DIM:s * WIDTH + (h + 1) * HEAD_DIM] = 1.0
    lane = np.arange(GW)
    dmat = (np.arange(CHUNK)[:, None] - (lane % HEAD_DIM)[None, :]).astype(np.int32)
    bdmask = ((lane[:, None] // HEAD_DIM) == (lane[None, :] // HEAD_DIM)).astype(np.float32)
    return (jnp.asarray(e64, BF16), jnp.asarray(tril, BF16), jnp.asarray(triu, BF16), jnp.asarray(expand, BF16),
            jnp.asarray(dmat), jnp.asarray(bdmask, BF16))


def _rope_tables(t):
    half = HEAD_DIM // 4
    rows = t // GRID_W
    d = jnp.arange(2 * HEAD_DIM) % HEAD_DIM
    freqs = ROPE_BASE ** (-(d % half).astype(F32) / half)
    by_row = (d < HEAD_DIM // 2)[None, None, :]
    ang_r = jnp.arange(rows).astype(F32)[:, None] * freqs[None, :]
    ang_c = jnp.arange(GRID_W).astype(F32)[:, None] * freqs[None, :]
    grid = lambda f: jnp.where(by_row, f(ang_r)[:, None, :], f(ang_c)[None, :, :]).reshape(t, 2 * HEAD_DIM)
    cos, sin = grid(jnp.cos), grid(jnp.sin)
    lower = ((d % (2 * half)) < half)[None, :]
    return cos, jnp.where(lower, -sin, 0.0), jnp.where(lower, 0.0, sin)


def _pad_lanes(a, offset):
    return jnp.zeros((1, 128), F32).at[0, offset:offset + a.size].set(a.reshape(-1))


def _layer(x, ctx, c, c_ctx, mod_w, mod_b, norm_w, w_in, conv_w, na_q_norm, na_k_norm, na_rpb, dn_A_log, dn_dt_bias,
           dn_norm_w, w_o_na, w_o_dn, w_out):
    b, t, d = x.shape
    w = WIDTH
    tm = 256
    e64, tril, triu, expand, dmat, bdmask = _constants(CUM_ROWS)

    cc = jnp.zeros((8, d), F32).at[:b].set(c).at[b].set(c_ctx)
    mod = _modulation(cc, mod_w, mod_b)
    shift, scale, gate = mod[:, :d], mod[:, d:2 * d], mod[:, 2 * d:]
    rows_x = lambda a: a[:b, None, :]
    rows_c = lambda a: jnp.broadcast_to(a[b][None, None, :], (b, 1, d))

    n_ba = 4 * HEADS
    ba0 = 8 * w
    w_main = jnp.concatenate([w_in[:, :ba0], w_in[:, ba0 + n_ba:]], axis=1).astype(BF16)
    w_ba = jnp.zeros((d, 128), F32).at[:, :n_ba].set(w_in[:, ba0:ba0 + n_ba]).astype(BF16)
    nw = norm_w.reshape(1, d)
    qw = jnp.tile(na_q_norm, HEADS).reshape(1, w)
    kw = jnp.tile(na_k_norm, HEADS).reshape(1, w)
    alog = _pad_lanes(dn_A_log, 2 * HEADS)
    dtb = _pad_lanes(dn_dt_bias, 2 * HEADS)
    project = functools.partial(_project, nw=nw, w_main=w_main, w_ba=w_ba, qw=qw, kw=kw, e64=e64, alog=alog,
                                dtb=dtb, conv_w=conv_w, tril=tril, triu=triu)
    q_na, k_na, v_na, gz_na, gz_dn, sg_na, sg_dn, *dn = project(x, rows_x(scale), rows_x(shift),
                                                                 tables=_rope_tables(t), tm=2 * tm)
    _, k_c, v_c, _, _, _, _, *dn_c = project(ctx, rows_c(scale), rows_c(shift), tables=None, tm=tm)

    bias = _bias_table(na_rpb, t // GRID_W)
    o_na = _neighbourhood_attention(q_na, k_na, v_na, k_c, v_c, bias)

    s_zero = jnp.zeros((b, 2, HEADS // GROUP, HEAD_DIM, GW), F32)
    _, _, s_ctx = _delta_scan(_chunk_terms(*dn_c, expand, dmat, bdmask, chunks=4), s_zero, bdmask)
    o_f, o_b, _ = _delta_scan(_chunk_terms(*dn, expand, dmat, bdmask, chunks=4), s_ctx, bdmask)

    dnw = jnp.tile(dn_norm_w, HEADS).reshape(1, w)
    return _merge(x, o_na, gz_na, o_f, o_b, gz_dn, sg_na, sg_dn, rows_x(gate), dnw, e64,
                  w_o_na.astype(BF16), w_o_dn.astype(BF16), w_out.astype(BF16), 4 * tm)


def kernel(x, c, ctx, c_ctx, mod_w, mod_b, norm_w, w_in, conv_w, na_q_norm, na_k_norm, na_rpb, dn_A_log, dn_dt_bias,
           dn_norm_w, w_o_na, w_o_dn, w_out):
    depth = mod_w.shape[0]
    assert depth == 1, "context-stream update between layers is not implemented"
    return _layer(x, ctx, c, c_ctx, mod_w[0], mod_b[0], norm_w[0], w_in[0], conv_w[0], na_q_norm[0], na_k_norm[0],
                  na_rpb[0], dn_A_log[0], dn_dt_bias[0], dn_norm_w[0], w_o_na[0], w_o_dn[0], w_out[0])
```

```python
import functools

import jax
import jax.numpy as jnp
import numpy as np
from jax import lax
from jax.experimental import pallas as pl
from jax.experimental.pallas import tpu as pltpu

F32 = jnp.float32
BF16 = jnp.bfloat16
EPS = 1e-6
GRID_W = 64
HEADS = 8
HEAD_DIM = 64
WIDTH = HEADS * HEAD_DIM
NA_KH = 8
NA_KW = 16
CHUNK = 64
ROPE_BASE = 10000.0
GROUP = 4
GW = GROUP * HEAD_DIM
NEG_INF = float("-inf")
LOG2E = 1.4426950408889634
HIGHEST = lax.Precision.HIGHEST
VMEM_LIMIT = 56 * 1024 * 1024


def _dot(a, b, **kw):
    return jnp.dot(a, b, preferred_element_type=F32, **kw)


def _dot_nt(a, b):
    return lax.dot_general(a, b, (((1,), (1,)), ((), ())), preferred_element_type=F32)


def _seg_sum(x, e_ref):
    xb = x.astype(BF16)
    e = e_ref[...]
    return jnp.concatenate([_dot(xb[:, g * GW:(g + 1) * GW], e) for g in range(x.shape[1] // GW)], axis=1)


def _silu(x):
    return x * jax.nn.sigmoid(x)


def _mod_body(c_ref, w_ref, b_ref, o_ref):
    o_ref[...] = _dot(_silu(c_ref[...]), w_ref[...], precision=HIGHEST) + b_ref[...]


def _modulation(cc, mod_w, mod_b):
    rows, d = cc.shape
    n = mod_w.shape[1]
    tn = 1024
    return pl.pallas_call(
        _mod_body,
        grid=(n // tn,),
        in_specs=[pl.BlockSpec((rows, d), lambda j: (0, 0)),
                  pl.BlockSpec((d, tn), lambda j: (0, j)),
                  pl.BlockSpec((1, tn), lambda j: (0, j))],
        out_specs=pl.BlockSpec((rows, tn), lambda j: (0, j)),
        out_shape=jax.ShapeDtypeStruct((rows, n), F32),
        compiler_params=pltpu.CompilerParams(dimension_semantics=("parallel",)),
        name="mod",
    )(cc, mod_w, mod_b.reshape(1, n))


CUM_ROWS = 256


def _split2(x):
    hi = x.astype(BF16)
    return hi, (x - hi.astype(F32)).astype(BF16)


def _proj_body(*refs, rope, latent):
    (x_ref, xp_ref, xn_ref, scale_ref, shift_ref, nw_ref, w_ref, wba_ref, qw_ref, kw_ref, e_ref, alog_ref, dtb_ref,
     cw_ref, tril_ref, triu_ref) = refs[:16]
    tabs = refs[16:19] if rope else ()
    kt_ref, v_ref, dq_ref, dk_ref, dv_ref, bgc_ref = refs[-6:]
    i = pl.program_id(1)
    last = pl.num_programs(1) - 1

    def modulated(xv):
        xn = xv * lax.rsqrt(jnp.mean(xv * xv, axis=-1, keepdims=True) + EPS)
        return ((xn * nw_ref[...]) * (1.0 + scale_ref[0]) + shift_ref[0]).astype(BF16)

    hb = modulated(x_ref[0])
    tm = hb.shape[0]
    halo = modulated(jnp.concatenate([xp_ref[0, 0], xn_ref[0, 0]], axis=0))

    def mm(lo, hi):
        return _dot(hb, w_ref[:, lo:hi])

    def head_rms(a, w_row):
        return a * lax.rsqrt(_seg_sum(a * a, e_ref) * (1.0 / HEAD_DIM) + EPS) * w_row

    w = WIDTH
    if latent:
        q_ref, gzna_ref, gzdn_ref, sgna_ref, sgdn_ref = refs[-11:-6]
        q_ref[0] = (head_rms(mm(0, w), qw_ref[...]) * (HEAD_DIM ** -0.5 * LOG2E)).astype(BF16)
    k_t = head_rms(mm(w, 2 * w), kw_ref[...]).T.astype(BF16)
    for j in range(kt_ref.shape[1]):
        kt_ref[0, j] = k_t[:, j * 128:(j + 1) * 128]
    v_ref[0] = mm(2 * w, 3 * w).astype(BF16)
    if latent:
        gzna_ref[0] = _silu(mm(3 * w, 4 * w)).astype(BF16)
        gzdn_ref[0] = _silu(mm(7 * w, 8 * w)).astype(BF16)
        sgna_ref[0] = jax.nn.sigmoid(mm(8 * w, 10 * w)).astype(BF16)
        sgdn_ref[0] = jax.nn.sigmoid(mm(10 * w, 12 * w)).astype(BF16)

    raw = _dot(jnp.concatenate([hb, halo], axis=0), w_ref[:, 4 * w:7 * w])
    xc = raw[:tm]
    before = jnp.where(i > 0, raw[tm + 7:tm + 8], 0.0)
    after = jnp.where(i < last, raw[tm + 8:tm + 9], 0.0)
    rowid = lax.broadcasted_iota(jnp.int32, xc.shape, 0)
    prev = jnp.where(rowid == 0, before, pltpu.roll(xc, 1, axis=0))
    nxt = jnp.where(rowid == tm - 1, after, pltpu.roll(xc, tm - 1, axis=0))
    y = _silu(prev * cw_ref[0:1, :] + xc * cw_ref[1:2, :] + nxt * cw_ref[2:3, :])
    half = HEAD_DIM // 4

    def norm_rope(a):
        a = a * lax.rsqrt(_seg_sum(a * a, e_ref) + EPS)
        if not rope:
            return a
        heads = lambda ref: jnp.concatenate([ref[...]] * (w // ref.shape[1]), axis=1)
        cos_ref, sina_ref, sinb_ref = tabs
        return (a * heads(cos_ref) + pltpu.roll(a, w - half, axis=1) * heads(sina_ref)
                + pltpu.roll(a, half, axis=1) * heads(sinb_ref))

    dq_ref[0] = norm_rope(y[:, :w]) * HEAD_DIM ** -0.5
    dk_ref[0] = norm_rope(y[:, w:2 * w])
    dv_ref[0] = y[:, 2 * w:]

    ba = _dot(hb, wba_ref[...])
    lane = lax.broadcasted_iota(jnp.int32, (CUM_ROWS, ba.shape[1]), 1)
    a = ba + dtb_ref[...]
    softplus = jnp.maximum(a, 0.0) + jnp.log1p(jnp.exp(-jnp.abs(a)))
    g = -jnp.exp(alog_ref[...]) * softplus
    beta = jax.nn.sigmoid(ba)
    n = ba.shape[1]
    for s in range(tm // CUM_ROWS):
        rows = slice(s * CUM_ROWS, (s + 1) * CUM_ROWS)
        parts = jnp.concatenate(_split2(g[rows]), axis=1)
        cum_f = _dot(tril_ref[...], parts)
        cum_b = _dot(triu_ref[...], parts)
        cum_f = (cum_f[:, :n] + cum_f[:, n:]) * LOG2E
        cum_b = (cum_b[:, :n] + cum_b[:, n:]) * LOG2E
        bgc_ref[0, rows] = jnp.where(lane < 2 * HEADS, beta[rows],
                                     jnp.where(lane < 3 * HEADS, cum_f, jnp.where(lane < 4 * HEADS, cum_b, 0.0)))


def _project(x, scale, shift, nw, w_main, w_ba, qw, kw, e64, alog, dtb, conv_w, tril, triu, tables, latent, tm):
    b, t, d = x.shape
    w = WIDTH
    per8 = tm // 8
    nblk8 = t // 8
    x8 = x.reshape(b, nblk8, 8, d)
    tok = lambda width: pl.BlockSpec((1, tm, width), lambda bi, i: (bi, i, 0))
    row = lambda width: pl.BlockSpec((1, width), lambda bi, i: (0, 0))
    per_batch = pl.BlockSpec((1, 1, d), lambda bi, i: (bi, 0, 0))
    full = lambda a: pl.BlockSpec(a.shape, lambda bi, i: (0, 0))
    tab = pl.BlockSpec((tm, 2 * HEAD_DIM), lambda bi, i: (i, 0))
    sds = lambda width, dt: jax.ShapeDtypeStruct((b, t, width), dt)
    tables = tuple(tables) if tables is not None else ()
    gate_specs = [tok(w), tok(w), tok(w), tok(2 * w), tok(2 * w)] if latent else []
    gate_shapes = [sds(w, BF16), sds(w, BF16), sds(w, BF16), sds(2 * w, BF16), sds(2 * w, BF16)] if latent else []
    return pl.pallas_call(
        functools.partial(_proj_body, rope=bool(tables), latent=latent),
        grid=(b, t // tm),
        in_specs=[tok(d),
                  pl.BlockSpec((1, 1, 8, d), lambda bi, i: (bi, jnp.maximum(i * per8 - 1, 0), 0, 0)),
                  pl.BlockSpec((1, 1, 8, d), lambda bi, i: (bi, jnp.minimum((i + 1) * per8, nblk8 - 1), 0, 0)),
                  per_batch, per_batch, row(d), full(w_main), full(w_ba), row(w), row(w), full(e64),
                  row(128), row(128), full(conv_w), full(tril), full(triu)] + [tab] * len(tables),
        out_specs=gate_specs + [pl.BlockSpec((1, tm // 128, w, 128), lambda bi, i: (bi, i, 0, 0)), tok(w), tok(w),
                                tok(w), tok(w), tok(128)],
        out_shape=gate_shapes + [jax.ShapeDtypeStruct((b, t // 128, w, 128), BF16), sds(w, BF16), sds(w, F32),
                                 sds(w, F32), sds(w, F32), sds(128, F32)],
        compiler_params=pltpu.CompilerParams(dimension_semantics=("parallel", "parallel"),
                                             vmem_limit_bytes=VMEM_LIMIT),
        name="proj",
    )(x, x8, x8, scale, shift, nw, w_main, w_ba, qw, kw, e64, alog, dtb, conv_w, tril, triu, *tables)


NA_BAND = NA_KH + 2
NA_MAX_PAIRS_PER_STEP = 16
NA_BAND_OFFSETS = NA_KH // 2 + 1


def _band_plan(rows):
    y = np.arange(NA_BAND)
    dy = np.zeros((NA_BAND_OFFSETS, 2, NA_BAND), np.int32)
    valid = np.zeros((NA_BAND_OFFSETS, 2, NA_BAND), bool)
    for var, m in enumerate((0, 1, 2, rows // 2 - 2, rows // 2 - 1)):
        b0 = min(max(2 * m - NA_KH // 2, 0), rows - NA_BAND)
        assert (2 * m - b0) // 2 == var
        for e in range(2):
            r = 2 * m + e
            r0 = min(max(r - NA_KH // 2, 0), rows - NA_KH)
            valid[var, e] = (b0 + y >= r0) & (b0 + y < r0 + NA_KH)
            dy[var, e] = np.clip(b0 + y - r + NA_KH - 1, 0, 2 * NA_KH - 2)
    return dy, valid


def _bias_body(rpb_ref, o_ref, tiles_ref, *, dy, valid):
    p = pl.program_id(0)
    n_dy, n_dx = 2 * NA_KH - 1, 2 * NA_KW - 1
    cq = lax.broadcasted_iota(jnp.int32, (GRID_W, GRID_W), 0)
    ck = lax.broadcasted_iota(jnp.int32, (GRID_W, GRID_W), 1)
    c0 = jnp.clip(cq - NA_KW // 2, 0, GRID_W - NA_KW)
    col_in = (ck >= c0) & (ck < c0 + NA_KW)
    dx = jnp.clip(ck - cq, -(NA_KW - 1), NA_KW - 1) + (NA_KW - 1)
    for hh in range(2):
        for i in range(n_dy):
            acc = jnp.zeros((GRID_W, GRID_W), F32)
            for d in range(n_dx):
                acc = jnp.where(dx == d, rpb_ref[((2 * p + hh) * n_dy + i) * n_dx + d], acc)
            tiles_ref[hh, i] = jnp.where(col_in, acc * LOG2E, NEG_INF)
    outside = jnp.full((GRID_W, GRID_W), NEG_INF, F32)
    for var in range(NA_BAND_OFFSETS):
        for e in range(2):
            for hh in range(2):
                r = (2 * e + hh) * GRID_W
                for y in range(NA_BAND):
                    tile = tiles_ref[hh, int(dy[var, e, y])] if valid[var, e, y] else outside
                    o_ref[0, var, r:r + GRID_W, y * GRID_W:(y + 1) * GRID_W] = tile


def _bias_table(rpb, rows):
    dy, valid = _band_plan(rows)
    shape = (HEADS // 2, NA_BAND_OFFSETS, 4 * GRID_W, NA_BAND * GRID_W)
    return pl.pallas_call(
        functools.partial(_bias_body, dy=dy, valid=valid),
        grid=(HEADS // 2,),
        in_specs=[pl.BlockSpec(memory_space=pltpu.SMEM)],
        out_specs=pl.BlockSpec((1,) + shape[1:], lambda p: (p, 0, 0, 0)),
        out_shape=jax.ShapeDtypeStruct(shape, F32),
        scratch_shapes=[pltpu.VMEM((2, 2 * NA_KH - 1, GRID_W, GRID_W), F32)],
        compiler_params=pltpu.CompilerParams(dimension_semantics=("parallel",)),
        name="bias",
    )(rpb.reshape(-1))


def _na_body(q_ref, kt_ref, v_ref, kct_ref, vc_ref, bias_ref, o_ref, *, rows, n_pairs):
    j = pl.program_id(2)
    lane = lax.broadcasted_iota(jnp.int32, (GRID_W, 2 * HEAD_DIM), 1)
    first = lane < HEAD_DIM
    kct = jnp.concatenate([kct_ref[0, i] for i in range(kct_ref.shape[1])], axis=1)
    with_ones = lambda v: jnp.concatenate([v, jnp.ones_like(v)], axis=1)
    vc = with_ones(vc_ref[0])
    tiles_per_band = NA_BAND * GRID_W // 128
    pairs = range(n_pairs)

    lhs, s_win, s_ctx, start = {}, {}, {}, {}
    for mm in pairs:
        m = j * n_pairs + mm
        b0 = jnp.clip(2 * m - NA_KH // 2, 0, rows - NA_BAND)
        variant = (2 * m - b0) // 2
        start[mm] = b0 // 2
        parts = []
        for e in range(2):
            qr = q_ref[0, (2 * mm + e) * GRID_W:(2 * mm + e + 1) * GRID_W, :]
            zero = jnp.zeros_like(qr)
            parts += [jnp.where(first, qr, zero), jnp.where(first, zero, qr)]
        lhs[mm] = jnp.concatenate(parts, axis=0)
        kt = jnp.concatenate([kt_ref[0, start[mm] + i] for i in range(tiles_per_band)], axis=1)
        s_win[mm] = _dot(lhs[mm], kt) + bias_ref[0, variant]
        s_ctx[mm] = _dot(lhs[mm], kct)
    p_win, p_ctx = {}, {}
    for mm in pairs:
        peak = jnp.maximum(jnp.max(s_win[mm], axis=-1, keepdims=True), jnp.max(s_ctx[mm], axis=-1, keepdims=True))
        p_win[mm] = jnp.exp2(s_win[mm] - peak).astype(BF16)
        p_ctx[mm] = jnp.exp2(s_ctx[mm] - peak).astype(BF16)
    for mm in pairs:
        vb = with_ones(v_ref[0, pl.ds(pl.multiple_of(start[mm] * 128, 128), NA_BAND * GRID_W), :])
        o = _dot(p_win[mm], vb) + _dot(p_ctx[mm], vc)
        o = o[:, :2 * HEAD_DIM] / o[:, 2 * HEAD_DIM:]
        for e in range(2):
            top = o[2 * e * GRID_W:(2 * e + 1) * GRID_W]
            bottom = o[(2 * e + 1) * GRID_W:(2 * e + 2) * GRID_W]
            o_ref[0, (2 * mm + e) * GRID_W:(2 * mm + e + 1) * GRID_W, :] = jnp.where(first, top, bottom).astype(
                o_ref.dtype)


def _neighbourhood_attention(q, kt, v, kct, vc, bias):
    b, t, _ = q.shape
    ctx_len = vc.shape[1]
    rows = t // GRID_W
    n_pairs = NA_MAX_PAIRS_PER_STEP
    while rows % (2 * n_pairs):
        n_pairs //= 2
    rows_per_step = 2 * n_pairs
    assert rows >= NA_BAND + 2 and rows % 2 == 0
    tq = rows_per_step * GRID_W
    pair_w = 2 * HEAD_DIM
    return pl.pallas_call(
        functools.partial(_na_body, rows=rows, n_pairs=n_pairs),
        grid=(b, HEADS // 2, t // tq),
        in_specs=[pl.BlockSpec((1, tq, pair_w), lambda bi, p, j: (bi, j, p)),
                  pl.BlockSpec((1, t // 128, pair_w, 128), lambda bi, p, j: (bi, 0, p, 0)),
                  pl.BlockSpec((1, t, pair_w), lambda bi, p, j: (bi, 0, p)),
                  pl.BlockSpec((1, ctx_len // 128, pair_w, 128), lambda bi, p, j: (bi, 0, p, 0)),
                  pl.BlockSpec((1, ctx_len, pair_w), lambda bi, p, j: (bi, 0, p)),
                  pl.BlockSpec((1,) + bias.shape[1:], lambda bi, p, j: (p, 0, 0, 0))],
        out_specs=pl.BlockSpec((1, tq, pair_w), lambda bi, p, j: (bi, j, p)),
        out_shape=jax.ShapeDtypeStruct((b, t, WIDTH), BF16),
        compiler_params=pltpu.CompilerParams(dimension_semantics=("parallel", "parallel", "arbitrary"),
                                             vmem_limit_bytes=VMEM_LIMIT),
        name="na",
    )(q, kt, v, kct, vc, bias)


def _block_diag(x, mask):
    return jnp.concatenate([x.astype(BF16)] * GROUP, axis=0) * mask


def _bdot(a, b):
    return _dot(a.astype(BF16), b.astype(BF16))


def _chunk_body(q_ref, k_ref, v_ref, bgc_ref, ex_ref, dmat_ref, mask_ref, *out_refs, chunks):
    dmat = dmat_ref[...]
    bdmask = mask_ref[...]
    eye = (dmat == 0).astype(F32)
    n_out = len(out_refs) // 2
    bd = lambda x: _block_diag(x, bdmask)
    tile = lambda ref, c, grp: ref[0, c * CHUNK:(c + 1) * CHUNK, grp * GW:(grp + 1) * GW]
    tiles = [(c, grp) for c in range(chunks) for grp in range(HEADS // GROUP)]
    units = [(c, grp, d) for c, grp in tiles for d in range(2)]

    hi, lo = _split2(bgc_ref[0])
    tc = hi.shape[0]
    beta_wide = _dot(hi, ex_ref[:, :2 * WIDTH])
    gc_wide = _dot(jnp.concatenate([hi, lo], axis=0), ex_ref[:, 2 * WIDTH:])
    gc_wide = gc_wide[:tc] + gc_wide[tc:]
    wide = lambda a, c, grp, d: a[c * CHUNK:(c + 1) * CHUNK, d * WIDTH + grp * GW:d * WIDTH + (grp + 1) * GW]

    both = {t: _dot_nt(jnp.concatenate([tile(k_ref, *t), tile(q_ref, *t)], axis=0).astype(BF16), bd(tile(k_ref, *t)))
            for t in tiles}

    row = lax.broadcasted_iota(jnp.int32, dmat.shape, 0)
    col = row - dmat
    same16, same32 = (row // 16) == (col // 16), (row // 32) == (col // 32)
    qk, inv, diag, off32, off64 = {}, {}, {}, {}, {}
    for c, grp, d in units:
        beta, gc = wide(beta_wide, c, grp, d), wide(gc_wide, c, grp, d)
        incl, strict = (dmat <= 0, dmat < 0) if d else (dmat >= 0, dmat > 0)
        g_row = jnp.sum(gc * eye, axis=0, keepdims=True)
        decay = jnp.exp2(jnp.where(incl, gc - g_row, NEG_INF))
        kk_qk = both[c, grp]
        low = jnp.where(strict, kk_qk[:CHUNK] * beta * decay, 0.0)
        qk[c, grp, d] = kk_qk[CHUNK:] * decay
        diag[c, grp, d] = jnp.where(same16, low, 0.0)
        off32[c, grp, d] = jnp.where(same32 & ~same16, low, 0.0)
        off64[c, grp, d] = jnp.where(same32, 0.0, low)
        inv[c, grp, d] = eye - diag[c, grp, d]

    power = {un: _bdot(diag[un], bd(diag[un])) for un in units}
    for _ in range(2):
        res = {un: _bdot(jnp.concatenate([power[un], inv[un]], axis=0), bd(power[un])) for un in units}
        power = {un: res[un][:CHUNK] for un in units}
        inv = {un: inv[un] + res[un][CHUNK:] for un in units}
    inv = {un: inv[un] + _bdot(inv[un], bd(power[un])) for un in units}
    for off in (off32, off64):
        half = {un: _bdot(inv[un], bd(off[un])) for un in units}
        inv = {un: inv[un] - _bdot(half[un], bd(inv[un])) for un in units}

    for c, grp, d in units:
        un = (c, grp, d)
        q, k, v = tile(q_ref, c, grp), tile(k_ref, c, grp), tile(v_ref, c, grp)
        beta, gc = wide(beta_wide, c, grp, d), wide(gc_wide, c, grp, d)
        last = 0 if d else CHUNK - 1
        eg = jnp.exp2(gc)
        g_last = gc[last:last + 1, :]
        u = _bdot(inv[un], bd(v * beta))
        wk = _bdot(inv[un], bd(k * beta * eg))
        k_dec = k * jnp.exp2(g_last - gc)
        u_ref, wk_ref, qk_ref, qd_ref, kdt_ref, gt_ref = out_refs[d * n_out:(d + 1) * n_out]
        rows, sl = slice(c * CHUNK, (c + 1) * CHUNK), slice(grp * GW, (grp + 1) * GW)
        u_ref[0, rows, sl] = u.astype(BF16)
        wk_ref[0, rows, sl] = wk.astype(BF16)
        qk_ref[0, rows, sl] = qk[un].astype(BF16)
        qd_ref[0, rows, sl] = (q * eg).astype(BF16)
        k_dec_t = k_dec.T.astype(BF16)
        for h in range(GROUP):
            kdt_ref[0, rows, grp * GW + h * HEAD_DIM:grp * GW + (h + 1) * HEAD_DIM] = (
                k_dec_t[h * HEAD_DIM:(h + 1) * HEAD_DIM, :])
        gt_ref[0, c, :, sl] = jnp.exp2(g_last)


def _chunk_terms(q, k, v, bgc, expand, dmat, bdmask, chunks):
    b, t, w = q.shape
    n = t // CHUNK
    tc = chunks * CHUNK
    tok = pl.BlockSpec((1, tc, w), lambda bi, i: (bi, i, 0))
    full = lambda a: pl.BlockSpec(a.shape, lambda bi, i: (0, 0))
    per_dir_specs = [tok] * 5 + [pl.BlockSpec((1, chunks, 1, w), lambda bi, i: (bi, i, 0, 0))]
    act = jax.ShapeDtypeStruct((b, t, w), BF16)
    per_dir_shapes = [act] * 5 + [jax.ShapeDtypeStruct((b, n, 1, w), F32)]
    return pl.pallas_call(
        functools.partial(_chunk_body, chunks=chunks),
        grid=(b, t // tc),
        in_specs=[tok] * 3 + [pl.BlockSpec((1, tc, 128), lambda bi, i: (bi, i, 0)), full(expand), full(dmat),
                  full(bdmask)],
        out_specs=per_dir_specs * 2,
        out_shape=per_dir_shapes * 2,
        compiler_params=pltpu.CompilerParams(dimension_semantics=("parallel", "parallel"),
                                             vmem_limit_bytes=VMEM_LIMIT),
        name="chunk",
    )(q, k, v, bgc, expand, dmat, bdmask)


SEQ_CHUNKS = 4


def _seq_body(*refs, batch):
    fwd, bwd = refs[0:6], refs[6:12]
    s0_ref, mask_ref, of_ref, ob_ref, s_ref = refs[12:]

    @pl.when(pl.program_id(0) == 0)
    def _():
        s_ref[...] = s0_ref[...]

    bdmask = mask_ref[...]
    dirs = ((fwd, of_ref), (bwd, ob_ref))
    units = [(b, d, grp) for b in range(batch) for d in range(2) for grp in range(HEADS // GROUP)]
    lanes = lambda grp: slice(grp * GW, (grp + 1) * GW)

    for sub in range(SEQ_CHUNKS):
        chunk_of = (sub, SEQ_CHUNKS - 1 - sub)
        rows = [slice(c * CHUNK, (c + 1) * CHUNK) for c in chunk_of]
        proj = {}
        for b, d, grp in units:
            _, wk_ref, _, qd_ref, _, _ = dirs[d][0]
            lhs = jnp.concatenate([wk_ref[b, rows[d], lanes(grp)], qd_ref[b, rows[d], lanes(grp)]], axis=0)
            proj[b, d, grp] = _dot(lhs, _block_diag(s_ref[b, d, grp], bdmask))
        for b, d, grp in units:
            (u_ref, _, qk_ref, _, kdt_ref, gt_ref), o_ref = dirs[d]
            v_new = u_ref[b, rows[d], lanes(grp)].astype(F32) - proj[b, d, grp][:CHUNK]
            lhs = jnp.concatenate([qk_ref[b, rows[d], lanes(grp)], kdt_ref[b, rows[d], lanes(grp)]], axis=0)
            res = _dot(lhs, _block_diag(v_new, bdmask))
            o_ref[b, rows[d], lanes(grp)] = (proj[b, d, grp][CHUNK:] + res[:CHUNK]).astype(o_ref.dtype)
            s_ref[b, d, grp] = s_ref[b, d, grp] * gt_ref[b, chunk_of[d], :, lanes(grp)] + res[CHUNK:]


def _delta_scan(terms, s0, bdmask):
    b, t, w = terms[0].shape
    steps = t // (SEQ_CHUNKS * CHUNK)

    def specs(idx):
        return ([pl.BlockSpec((b, SEQ_CHUNKS * CHUNK, w), lambda i: (0, idx(i), 0))] * 5
                + [pl.BlockSpec((b, SEQ_CHUNKS, 1, w), lambda i: (0, idx(i), 0, 0))])

    forward, backward = (lambda i: i), (lambda i: steps - 1 - i)
    st = pl.BlockSpec(s0.shape, lambda i: (0,) * s0.ndim)
    o = jax.ShapeDtypeStruct((b, t, w), BF16)
    return pl.pallas_call(
        functools.partial(_seq_body, batch=b),
        grid=(steps,),
        in_specs=specs(forward) + specs(backward) + [st, pl.BlockSpec(bdmask.shape, lambda i: (0, 0))],
        out_specs=[specs(forward)[0], specs(backward)[0], st],
        out_shape=[o, o, jax.ShapeDtypeStruct(s0.shape, F32)],
        compiler_params=pltpu.CompilerParams(dimension_semantics=("arbitrary",), vmem_limit_bytes=VMEM_LIMIT),
        name="seq",
    )(*terms, s0, bdmask)


def _merge_body(x_ref, ona_ref, gzna_ref, of_ref, ob_ref, gzdn_ref, sgna_ref, sgdn_ref, gate_ref, dnw_ref, e_ref,
                wna_ref, wdn_ref, wout_ref, o_ref):
    a = (ona_ref[0].astype(F32) * gzna_ref[0].astype(F32)).astype(BF16)
    u_na = _dot(a, wna_ref[...])
    od = of_ref[0].astype(F32) + ob_ref[0].astype(F32)
    odn = od * lax.rsqrt(_seg_sum(od * od, e_ref) * (1.0 / HEAD_DIM) + EPS) * dnw_ref[...]
    u_dn = _dot((odn * gzdn_ref[0].astype(F32)).astype(BF16), wdn_ref[...])
    y = sgna_ref[0].astype(F32) * u_na + sgdn_ref[0].astype(F32) * u_dn
    o_ref[0] = x_ref[0] + gate_ref[0] * _dot(y.astype(BF16), wout_ref[...])


def _merge(x, o_na, gz_na, o_f, o_b, gz_dn, sg_na, sg_dn, gate, dnw, e64, w_o_na, w_o_dn, w_out, tm):
    b, t, d = x.shape
    w = WIDTH
    tok = lambda width: pl.BlockSpec((1, tm, width), lambda bi, i: (bi, i, 0))
    full = lambda a: pl.BlockSpec(a.shape, lambda bi, i: (0, 0))
    return pl.pallas_call(
        _merge_body,
        grid=(b, t // tm),
        in_specs=[tok(d), tok(w), tok(w), tok(w), tok(w), tok(w), tok(d), tok(d),
                  pl.BlockSpec((1, 1, d), lambda bi, i: (bi, 0, 0)), full(dnw), full(e64),
                  full(w_o_na), full(w_o_dn), full(w_out)],
        out_specs=tok(d),
        out_shape=jax.ShapeDtypeStruct((b, t, d), F32),
        compiler_params=pltpu.CompilerParams(dimension_semantics=("parallel", "parallel"),
                                             vmem_limit_bytes=VMEM_LIMIT),
        name="merge",
    )(x, o_na, gz_na, o_f, o_b, gz_dn, sg_na, sg_dn, gate, dnw, e64, w_o_na, w_o_dn, w_out)


def _constants(tc):
    seg = np.arange(GW) // HEAD_DIM
    e64 = (seg[:, None] == seg[None, :]).astype(np.float32)
    tok = np.arange(tc)
    same_chunk = (tok[:, None] // CHUNK) == (tok[None, :] // CHUNK)
    tril = (same_chunk & (tok[None, :] <= tok[:, None])).astype(np.float32)
    triu = (same_chunk & (tok[None, :] >= tok[:, None])).astype(np.float32)
    expand = np.zeros((128, 4 * WIDTH), np.float32)
    for s in range(4):
        for h in range(HEADS):
            expand[s * HEADS + h, s * WIDTH + h * HEAD_DIM:s * WIDTH + (h + 1) * HEAD_DIM] = 1.0
    lane = np.arange(GW)
    dmat = (np.arange(CHUNK)[:, None] - (lane % HEAD_DIM)[None, :]).astype(np.int32)
    bdmask = ((lane[:, None] // HEAD_DIM) == (lane[None, :] // HEAD_DIM)).astype(np.float32)
    return (jnp.asarray(e64, BF16), jnp.asarray(tril, BF16), jnp.asarray(triu, BF16), jnp.asarray(expand, BF16),
            jnp.asarray(dmat), jnp.asarray(bdmask, BF16))


def _rope_tables(t):
    half = HEAD_DIM // 4
    rows = t // GRID_W
    d = jnp.arange(2 * HEAD_DIM) % HEAD_DIM
    freqs = ROPE_BASE ** (-(d % half).astype(F32) / half)
    by_row = (d < HEAD_DIM // 2)[None, None, :]
    ang_r = jnp.arange(rows).astype(F32)[:, None] * freqs[None, :]
    ang_c = jnp.arange(GRID_W).astype(F32)[:, None] * freqs[None, :]
    grid = lambda f: jnp.where(by_row, f(ang_r)[:, None, :], f(ang_c)[None, :, :]).reshape(t, 2 * HEAD_DIM)
    cos, sin = grid(jnp.cos), grid(jnp.sin)
    lower = ((d % (2 * half)) < half)[None, :]
    return cos, jnp.where(lower, -sin, 0.0), jnp.where(lower, 0.0, sin)


def _pad_lanes(a, offset):
    return jnp.zeros((1, 128), F32).at[0, offset:offset + a.size].set(a.reshape(-1))


def _layer(x, ctx, c, c_ctx, mod_w, mod_b, norm_w, w_in, conv_w, na_q_norm, na_k_norm, na_rpb, dn_A_log, dn_dt_bias,
           dn_norm_w, w_o_na, w_o_dn, w_out):
    b, t, d = x.shape
    w = WIDTH
    tm = 256
    e64, tril, triu, expand, dmat, bdmask = _constants(CUM_ROWS)

    cc = jnp.zeros((8, d), F32).at[:b].set(c).at[b].set(c_ctx)
    mod = _modulation(cc, mod_w, mod_b)
    shift, scale, gate = mod[:, :d], mod[:, d:2 * d], mod[:, 2 * d:]
    rows_x = lambda a: a[:b, None, :]
    rows_c = lambda a: jnp.broadcast_to(a[b][None, None, :], (b, 1, d))

    n_ba = 4 * HEADS
    ba0 = 8 * w
    w_main = jnp.concatenate([w_in[:, :ba0], w_in[:, ba0 + n_ba:]], axis=1).astype(BF16)
    w_ba = jnp.zeros((d, 128), F32).at[:, :n_ba].set(w_in[:, ba0:ba0 + n_ba]).astype(BF16)
    nw = norm_w.reshape(1, d)
    qw = jnp.tile(na_q_norm, HEADS).reshape(1, w)
    kw = jnp.tile(na_k_norm, HEADS).reshape(1, w)
    alog = _pad_lanes(dn_A_log, 2 * HEADS)
    dtb = _pad_lanes(dn_dt_bias, 2 * HEADS)
    project = functools.partial(_project, nw=nw, w_main=w_main, w_ba=w_ba, qw=qw, kw=kw, e64=e64, alog=alog,
                                dtb=dtb, conv_w=conv_w, tril=tril, triu=triu)
    q_na, gz_na, gz_dn, sg_na, sg_dn, k_na, v_na, *dn = project(x, rows_x(scale), rows_x(shift),
                                                                 tables=_rope_tables(t), latent=True, tm=2 * tm)
    k_c, v_c, *dn_c = project(ctx, rows_c(scale), rows_c(shift), tables=None, latent=False, tm=tm)

    bias = _bias_table(na_rpb, t // GRID_W)
    o_na = _neighbourhood_attention(q_na, k_na, v_na, k_c, v_c, bias)

    s_zero = jnp.zeros((b, 2, HEADS // GROUP, HEAD_DIM, GW), F32)
    _, _, s_ctx = _delta_scan(_chunk_terms(*dn_c, expand, dmat, bdmask, chunks=4), s_zero, bdmask)
    o_f, o_b, _ = _delta_scan(_chunk_terms(*dn, expand, dmat, bdmask, chunks=8), s_ctx, bdmask)

    dnw = jnp.tile(dn_norm_w, HEADS).reshape(1, w)
    return _merge(x, o_na, gz_na, o_f, o_b, gz_dn, sg_na, sg_dn, rows_x(gate), dnw, e64,
                  w_o_na.astype(BF16), w_o_dn.astype(BF16), w_out.astype(BF16), 4 * tm)


def kernel(x, c, ctx, c_ctx, mod_w, mod_b, norm_w, w_in, conv_w, na_q_norm, na_k_norm, na_rpb, dn_A_log, dn_dt_bias,
           dn_norm_w, w_o_na, w_o_dn, w_out):
    depth = mod_w.shape[0]
    assert depth == 1, "context-stream update between layers is not implemented"
    return _layer(x, ctx, c, c_ctx, mod_w[0], mod_b[0], norm_w[0], w_in[0], conv_w[0], na_q_norm[0], na_k_norm[0],
                  na_rpb[0], dn_A_log[0], dn_dt_bias[0], dn_norm_w[0], w_o_na[0], w_o_dn[0], w_out[0])
```

```python
import functools

import jax
import jax.numpy as jnp
import numpy as np
from jax import lax
from jax.experimental import pallas as pl
from jax.experimental.pallas import tpu as pltpu

F32 = jnp.float32
BF16 = jnp.bfloat16
EPS = 1e-6
GRID_W = 64
HEADS = 8
HEAD_DIM = 64
WIDTH = HEADS * HEAD_DIM
NA_KH = 8
NA_KW = 16
CHUNK = 64
ROPE_BASE = 10000.0
GROUP = 4
GW = GROUP * HEAD_DIM
NEG_INF = float("-inf")
LOG2E = 1.4426950408889634
HIGHEST = lax.Precision.HIGHEST
VMEM_LIMIT = 56 * 1024 * 1024


def _dot(a, b, **kw):
    return jnp.dot(a, b, preferred_element_type=F32, **kw)


def _dot_nt(a, b):
    return lax.dot_general(a, b, (((1,), (1,)), ((), ())), preferred_element_type=F32)


def _seg_sum(x, e_ref):
    xb = x.astype(BF16)
    e = e_ref[...]
    return jnp.concatenate([_dot(xb[:, g * GW:(g + 1) * GW], e) for g in range(x.shape[1] // GW)], axis=1)


def _silu(x):
    return x * jax.nn.sigmoid(x)


def _mod_body(c_ref, w_ref, b_ref, o_ref):
    o_ref[...] = _dot(_silu(c_ref[...]), w_ref[...], precision=HIGHEST) + b_ref[...]


def _modulation(cc, mod_w, mod_b):
    rows, d = cc.shape
    n = mod_w.shape[1]
    tn = 1024
    return pl.pallas_call(
        _mod_body,
        grid=(n // tn,),
        in_specs=[pl.BlockSpec((rows, d), lambda j: (0, 0)),
                  pl.BlockSpec((d, tn), lambda j: (0, j)),
                  pl.BlockSpec((1, tn), lambda j: (0, j))],
        out_specs=pl.BlockSpec((rows, tn), lambda j: (0, j)),
        out_shape=jax.ShapeDtypeStruct((rows, n), F32),
        compiler_params=pltpu.CompilerParams(dimension_semantics=("parallel",)),
        name="mod",
    )(cc, mod_w, mod_b.reshape(1, n))


CUM_ROWS = 256


def _split2(x):
    hi = x.astype(BF16)
    return hi, (x - hi.astype(F32)).astype(BF16)


def _proj_body(*refs, rope, latent):
    (x_ref, xp_ref, xn_ref, scale_ref, shift_ref, nw_ref, w_ref, wba_ref, wt_ref, qw_ref, kw_ref, e_ref, alog_ref,
     dtb_ref, cw_ref, tril_ref, triu_ref) = refs[:17]
    tabs = refs[17:20] if rope else ()
    kt_ref, v_ref, dq_ref, dk_ref, dv_ref, bgc_ref = refs[-6:]
    i = pl.program_id(1)
    last = pl.num_programs(1) - 1

    def modulated(xv):
        xn = xv * lax.rsqrt(jnp.mean(xv * xv, axis=-1, keepdims=True) + EPS)
        return ((xn * nw_ref[...]) * (1.0 + scale_ref[0]) + shift_ref[0]).astype(BF16)

    hb = modulated(x_ref[0])
    tm = hb.shape[0]
    halo = modulated(jnp.concatenate([xp_ref[0, 0], xn_ref[0, 0]], axis=0))

    split = w_ref.shape[1]

    def mm(lo, hi):
        return _dot(hb, w_ref[:, lo:hi] if hi <= split else wt_ref[:, lo - split:hi - split])

    def head_rms(a, w_row):
        return a * lax.rsqrt(_seg_sum(a * a, e_ref) * (1.0 / HEAD_DIM) + EPS) * w_row

    w = WIDTH
    if latent:
        q_ref, gzna_ref, gzdn_ref, sgna_ref, sgdn_ref = refs[-11:-6]
        q_ref[0] = (head_rms(mm(0, w), qw_ref[...]) * (HEAD_DIM ** -0.5 * LOG2E)).astype(BF16)
    k_t = head_rms(mm(w, 2 * w), kw_ref[...]).T.astype(BF16)
    for j in range(kt_ref.shape[1]):
        kt_ref[0, j] = k_t[:, j * 128:(j + 1) * 128]
    v_ref[0] = mm(2 * w, 3 * w).astype(BF16)
    if latent:
        gzna_ref[0] = _silu(mm(3 * w, 4 * w)).astype(BF16)
        gzdn_ref[0] = _silu(mm(7 * w, 8 * w)).astype(BF16)
        sgna_ref[0] = jax.nn.sigmoid(mm(8 * w, 10 * w)).astype(BF16)
        sgdn_ref[0] = jax.nn.sigmoid(mm(10 * w, 12 * w)).astype(BF16)

    raw = _dot(jnp.concatenate([hb, halo], axis=0), w_ref[:, 4 * w:7 * w])
    xc = raw[:tm]
    before = jnp.where(i > 0, raw[tm + 7:tm + 8], 0.0)
    after = jnp.where(i < last, raw[tm + 8:tm + 9], 0.0)
    rowid = lax.broadcasted_iota(jnp.int32, xc.shape, 0)
    prev = jnp.where(rowid == 0, before, pltpu.roll(xc, 1, axis=0))
    nxt = jnp.where(rowid == tm - 1, after, pltpu.roll(xc, tm - 1, axis=0))
    y = _silu(prev * cw_ref[0:1, :] + xc * cw_ref[1:2, :] + nxt * cw_ref[2:3, :])
    half = HEAD_DIM // 4

    def norm_rope(a):
        a = a * lax.rsqrt(_seg_sum(a * a, e_ref) + EPS)
        if not rope:
            return a
        heads = lambda ref: jnp.concatenate([ref[...]] * (w // ref.shape[1]), axis=1)
        cos_ref, sina_ref, sinb_ref = tabs
        return (a * heads(cos_ref) + pltpu.roll(a, w - half, axis=1) * heads(sina_ref)
                + pltpu.roll(a, half, axis=1) * heads(sinb_ref))

    dq_ref[0] = norm_rope(y[:, :w]) * HEAD_DIM ** -0.5
    dk_ref[0] = norm_rope(y[:, w:2 * w])
    dv_ref[0] = y[:, 2 * w:]

    ba = _dot(hb, wba_ref[...])
    lane = lax.broadcasted_iota(jnp.int32, (CUM_ROWS, ba.shape[1]), 1)
    a = ba + dtb_ref[...]
    softplus = jnp.maximum(a, 0.0) + jnp.log1p(jnp.exp(-jnp.abs(a)))
    g = -jnp.exp(alog_ref[...]) * softplus
    beta = jax.nn.sigmoid(ba)
    n = ba.shape[1]
    for s in range(tm // CUM_ROWS):
        rows = slice(s * CUM_ROWS, (s + 1) * CUM_ROWS)
        parts = jnp.concatenate(_split2(g[rows]), axis=1)
        cum_f = _dot(tril_ref[...], parts)
        cum_b = _dot(triu_ref[...], parts)
        cum_f = (cum_f[:, :n] + cum_f[:, n:]) * LOG2E
        cum_b = (cum_b[:, :n] + cum_b[:, n:]) * LOG2E
        bgc_ref[0, rows] = jnp.where(lane < 2 * HEADS, beta[rows],
                                     jnp.where(lane < 3 * HEADS, cum_f, jnp.where(lane < 4 * HEADS, cum_b, 0.0)))


def _project(x, scale, shift, nw, w_all, w_tail, qw, kw, e64, alog, dtb, conv_w, tril, triu, tables, latent, tm):
    b, t, d = x.shape
    w = WIDTH
    per8 = tm // 8
    nblk8 = t // 8
    x8 = x.reshape(b, nblk8, 8, d)
    tok = lambda width: pl.BlockSpec((1, tm, width), lambda bi, i: (bi, i, 0))
    row = lambda width: pl.BlockSpec((1, width), lambda bi, i: (0, 0))
    per_batch = pl.BlockSpec((1, 1, d), lambda bi, i: (bi, 0, 0))
    full = lambda a: pl.BlockSpec(a.shape, lambda bi, i: (0, 0))
    tab = pl.BlockSpec((tm, 2 * HEAD_DIM), lambda bi, i: (i, 0))
    sds = lambda width, dt: jax.ShapeDtypeStruct((b, t, width), dt)
    tables = tuple(tables) if tables is not None else ()
    gate_specs = [tok(w), tok(w), tok(w), tok(2 * w), tok(2 * w)] if latent else []
    gate_shapes = [sds(w, BF16), sds(w, BF16), sds(w, BF16), sds(2 * w, BF16), sds(2 * w, BF16)] if latent else []
    return pl.pallas_call(
        functools.partial(_proj_body, rope=bool(tables), latent=latent),
        grid=(b, t // tm),
        in_specs=[tok(d),
                  pl.BlockSpec((1, 1, 8, d), lambda bi, i: (bi, jnp.maximum(i * per8 - 1, 0), 0, 0)),
                  pl.BlockSpec((1, 1, 8, d), lambda bi, i: (bi, jnp.minimum((i + 1) * per8, nblk8 - 1), 0, 0)),
                  per_batch, per_batch, row(d), pl.BlockSpec((d, 8 * w), lambda bi, i: (0, 0)),
                  pl.BlockSpec((d, 128), lambda bi, i: (0, 8 * w // 128)), full(w_tail), row(w), row(w), full(e64),
                  row(128), row(128), full(conv_w), full(tril), full(triu)] + [tab] * len(tables),
        out_specs=gate_specs + [pl.BlockSpec((1, tm // 128, w, 128), lambda bi, i: (bi, i, 0, 0)), tok(w), tok(w),
                                tok(w), tok(w), tok(128)],
        out_shape=gate_shapes + [jax.ShapeDtypeStruct((b, t // 128, w, 128), BF16), sds(w, BF16), sds(w, F32),
                                 sds(w, F32), sds(w, F32), sds(128, F32)],
        compiler_params=pltpu.CompilerParams(dimension_semantics=("parallel", "parallel"),
                                             vmem_limit_bytes=VMEM_LIMIT),
        name="proj",
    )(x, x8, x8, scale, shift, nw, w_all, w_all, w_tail, qw, kw, e64, alog, dtb, conv_w, tril, triu, *tables)


NA_BAND = NA_KH + 2
NA_MAX_PAIRS_PER_STEP = 16
NA_BAND_OFFSETS = NA_KH // 2 + 1


def _band_plan(rows):
    y = np.arange(NA_BAND)
    dy = np.zeros((NA_BAND_OFFSETS, 2, NA_BAND), np.int32)
    valid = np.zeros((NA_BAND_OFFSETS, 2, NA_BAND), bool)
    for var, m in enumerate((0, 1, 2, rows // 2 - 2, rows // 2 - 1)):
        b0 = min(max(2 * m - NA_KH // 2, 0), rows - NA_BAND)
        assert (2 * m - b0) // 2 == var
        for e in range(2):
            r = 2 * m + e
            r0 = min(max(r - NA_KH // 2, 0), rows - NA_KH)
            valid[var, e] = (b0 + y >= r0) & (b0 + y < r0 + NA_KH)
            dy[var, e] = np.clip(b0 + y - r + NA_KH - 1, 0, 2 * NA_KH - 2)
    return dy, valid


def _bias_body(rpb_ref, o_ref, tiles_ref, *, dy, valid):
    p = pl.program_id(0)
    n_dy, n_dx = 2 * NA_KH - 1, 2 * NA_KW - 1
    cq = lax.broadcasted_iota(jnp.int32, (GRID_W, GRID_W), 0)
    ck = lax.broadcasted_iota(jnp.int32, (GRID_W, GRID_W), 1)
    c0 = jnp.clip(cq - NA_KW // 2, 0, GRID_W - NA_KW)
    col_in = (ck >= c0) & (ck < c0 + NA_KW)
    dx = jnp.clip(ck - cq, -(NA_KW - 1), NA_KW - 1) + (NA_KW - 1)
    for hh in range(2):
        for i in range(n_dy):
            acc = jnp.zeros((GRID_W, GRID_W), F32)
            for d in range(n_dx):
                acc = jnp.where(dx == d, rpb_ref[((2 * p + hh) * n_dy + i) * n_dx + d], acc)
            tiles_ref[hh, i] = jnp.where(col_in, acc * LOG2E, NEG_INF)
    outside = jnp.full((GRID_W, GRID_W), NEG_INF, F32)
    for var in range(NA_BAND_OFFSETS):
        for e in range(2):
            for hh in range(2):
                r = (2 * e + hh) * GRID_W
                for y in range(NA_BAND):
                    tile = tiles_ref[hh, int(dy[var, e, y])] if valid[var, e, y] else outside
                    o_ref[0, var, r:r + GRID_W, y * GRID_W:(y + 1) * GRID_W] = tile


def _bias_table(rpb, rows):
    dy, valid = _band_plan(rows)
    shape = (HEADS // 2, NA_BAND_OFFSETS, 4 * GRID_W, NA_BAND * GRID_W)
    return pl.pallas_call(
        functools.partial(_bias_body, dy=dy, valid=valid),
        grid=(HEADS // 2,),
        in_specs=[pl.BlockSpec(memory_space=pltpu.SMEM)],
        out_specs=pl.BlockSpec((1,) + shape[1:], lambda p: (p, 0, 0, 0)),
        out_shape=jax.ShapeDtypeStruct(shape, F32),
        scratch_shapes=[pltpu.VMEM((2, 2 * NA_KH - 1, GRID_W, GRID_W), F32)],
        compiler_params=pltpu.CompilerParams(dimension_semantics=("parallel",)),
        name="bias",
    )(rpb.reshape(-1))


def _na_body(q_ref, kt_ref, v_ref, kct_ref, vc_ref, bias_ref, o_ref, *, rows, n_pairs):
    j = pl.program_id(2)
    lane = lax.broadcasted_iota(jnp.int32, (GRID_W, 2 * HEAD_DIM), 1)
    first = lane < HEAD_DIM
    kct = jnp.concatenate([kct_ref[0, i] for i in range(kct_ref.shape[1])], axis=1)
    with_ones = lambda v: jnp.concatenate([v, jnp.ones_like(v)], axis=1)
    vc = with_ones(vc_ref[0])
    tiles_per_band = NA_BAND * GRID_W // 128
    pairs = range(n_pairs)

    lhs, s_win, s_ctx, start = {}, {}, {}, {}
    for mm in pairs:
        m = j * n_pairs + mm
        b0 = jnp.clip(2 * m - NA_KH // 2, 0, rows - NA_BAND)
        variant = (2 * m - b0) // 2
        start[mm] = b0 // 2
        parts = []
        for e in range(2):
            qr = q_ref[0, (2 * mm + e) * GRID_W:(2 * mm + e + 1) * GRID_W, :]
            zero = jnp.zeros_like(qr)
            parts += [jnp.where(first, qr, zero), jnp.where(first, zero, qr)]
        lhs[mm] = jnp.concatenate(parts, axis=0)
        kt = jnp.concatenate([kt_ref[0, start[mm] + i] for i in range(tiles_per_band)], axis=1)
        s_win[mm] = _dot(lhs[mm], kt) + bias_ref[0, variant]
        s_ctx[mm] = _dot(lhs[mm], kct)
    p_win, p_ctx = {}, {}
    for mm in pairs:
        peak = jnp.maximum(jnp.max(s_win[mm], axis=-1, keepdims=True), jnp.max(s_ctx[mm], axis=-1, keepdims=True))
        p_win[mm] = jnp.exp2(s_win[mm] - peak).astype(BF16)
        p_ctx[mm] = jnp.exp2(s_ctx[mm] - peak).astype(BF16)
    for mm in pairs:
        vb = with_ones(v_ref[0, pl.ds(pl.multiple_of(start[mm] * 128, 128), NA_BAND * GRID_W), :])
        o = _dot(p_win[mm], vb) + _dot(p_ctx[mm], vc)
        o = o[:, :2 * HEAD_DIM] / o[:, 2 * HEAD_DIM:]
        for e in range(2):
            top = o[2 * e * GRID_W:(2 * e + 1) * GRID_W]
            bottom = o[(2 * e + 1) * GRID_W:(2 * e + 2) * GRID_W]
            o_ref[0, (2 * mm + e) * GRID_W:(2 * mm + e + 1) * GRID_W, :] = jnp.where(first, top, bottom).astype(
                o_ref.dtype)


def _neighbourhood_attention(q, kt, v, kct, vc, bias):
    b, t, _ = q.shape
    ctx_len = vc.shape[1]
    rows = t // GRID_W
    n_pairs = NA_MAX_PAIRS_PER_STEP
    while rows % (2 * n_pairs):
        n_pairs //= 2
    rows_per_step = 2 * n_pairs
    assert rows >= NA_BAND + 2 and rows % 2 == 0
    tq = rows_per_step * GRID_W
    pair_w = 2 * HEAD_DIM
    return pl.pallas_call(
        functools.partial(_na_body, rows=rows, n_pairs=n_pairs),
        grid=(b, HEADS // 2, t // tq),
        in_specs=[pl.BlockSpec((1, tq, pair_w), lambda bi, p, j: (bi, j, p)),
                  pl.BlockSpec((1, t // 128, pair_w, 128), lambda bi, p, j: (bi, 0, p, 0)),
                  pl.BlockSpec((1, t, pair_w), lambda bi, p, j: (bi, 0, p)),
                  pl.BlockSpec((1, ctx_len // 128, pair_w, 128), lambda bi, p, j: (bi, 0, p, 0)),
                  pl.BlockSpec((1, ctx_len, pair_w), lambda bi, p, j: (bi, 0, p)),
                  pl.BlockSpec((1,) + bias.shape[1:], lambda bi, p, j: (p, 0, 0, 0))],
        out_specs=pl.BlockSpec((1, tq, pair_w), lambda bi, p, j: (bi, j, p)),
        out_shape=jax.ShapeDtypeStruct((b, t, WIDTH), BF16),
        compiler_params=pltpu.CompilerParams(dimension_semantics=("parallel", "parallel", "arbitrary"),
                                             vmem_limit_bytes=VMEM_LIMIT),
        name="na",
    )(q, kt, v, kct, vc, bias)


def _block_diag(x, mask):
    return jnp.concatenate([x.astype(BF16)] * GROUP, axis=0) * mask


def _bdot(a, b):
    return _dot(a.astype(BF16), b.astype(BF16))


def _chunk_body(q_ref, k_ref, v_ref, bgc_ref, ex_ref, dmat_ref, mask_ref, *out_refs, chunks):
    dmat = dmat_ref[...]
    bdmask = mask_ref[...]
    eye = (dmat == 0).astype(F32)
    n_out = len(out_refs) // 2
    bd = lambda x: _block_diag(x, bdmask)
    tile = lambda ref, c, grp: ref[0, c * CHUNK:(c + 1) * CHUNK, grp * GW:(grp + 1) * GW]
    tiles = [(c, grp) for c in range(chunks) for grp in range(HEADS // GROUP)]
    units = [(c, grp, d) for c, grp in tiles for d in range(2)]

    hi, lo = _split2(bgc_ref[0])
    tc = hi.shape[0]
    beta_wide = _dot(hi, ex_ref[:, :2 * WIDTH])
    gc_wide = _dot(jnp.concatenate([hi, lo], axis=0), ex_ref[:, 2 * WIDTH:])
    gc_wide = gc_wide[:tc] + gc_wide[tc:]
    wide = lambda a, c, grp, d: a[c * CHUNK:(c + 1) * CHUNK, d * WIDTH + grp * GW:d * WIDTH + (grp + 1) * GW]

    both = {t: _dot_nt(jnp.concatenate([tile(k_ref, *t), tile(q_ref, *t)], axis=0).astype(BF16), bd(tile(k_ref, *t)))
            for t in tiles}

    row = lax.broadcasted_iota(jnp.int32, dmat.shape, 0)
    col = row - dmat
    same16, same32 = (row // 16) == (col // 16), (row // 32) == (col // 32)
    qk, inv, diag, off32, off64 = {}, {}, {}, {}, {}
    for c, grp, d in units:
        beta, gc = wide(beta_wide, c, grp, d), wide(gc_wide, c, grp, d)
        incl, strict = (dmat <= 0, dmat < 0) if d else (dmat >= 0, dmat > 0)
        g_row = jnp.sum(gc * eye, axis=0, keepdims=True)
        decay = jnp.exp2(jnp.where(incl, gc - g_row, NEG_INF))
        kk_qk = both[c, grp]
        low = jnp.where(strict, kk_qk[:CHUNK] * beta * decay, 0.0)
        qk[c, grp, d] = kk_qk[CHUNK:] * decay
        diag[c, grp, d] = jnp.where(same16, low, 0.0)
        off32[c, grp, d] = jnp.where(same32 & ~same16, low, 0.0)
        off64[c, grp, d] = jnp.where(same32, 0.0, low)
        inv[c, grp, d] = eye - diag[c, grp, d]

    power = {un: _bdot(diag[un], bd(diag[un])) for un in units}
    for _ in range(2):
        res = {un: _bdot(jnp.concatenate([power[un], inv[un]], axis=0), bd(power[un])) for un in units}
        power = {un: res[un][:CHUNK] for un in units}
        inv = {un: inv[un] + res[un][CHUNK:] for un in units}
    inv = {un: inv[un] + _bdot(inv[un], bd(power[un])) for un in units}
    for off in (off32, off64):
        half = {un: _bdot(inv[un], bd(off[un])) for un in units}
        inv = {un: inv[un] - _bdot(half[un], bd(inv[un])) for un in units}

    for c, grp, d in units:
        un = (c, grp, d)
        q, k, v = tile(q_ref, c, grp), tile(k_ref, c, grp), tile(v_ref, c, grp)
        beta, gc = wide(beta_wide, c, grp, d), wide(gc_wide, c, grp, d)
        last = 0 if d else CHUNK - 1
        eg = jnp.exp2(gc)
        g_last = gc[last:last + 1, :]
        u = _bdot(inv[un], bd(v * beta))
        wk = _bdot(inv[un], bd(k * beta * eg))
        k_dec = k * jnp.exp2(g_last - gc)
        u_ref, wk_ref, qk_ref, qd_ref, kdt_ref, gt_ref = out_refs[d * n_out:(d + 1) * n_out]
        rows, sl = slice(c * CHUNK, (c + 1) * CHUNK), slice(grp * GW, (grp + 1) * GW)
        u_ref[0, rows, sl] = u.astype(BF16)
        wk_ref[0, rows, sl] = wk.astype(BF16)
        qk_ref[0, rows, sl] = qk[un].astype(BF16)
        qd_ref[0, rows, sl] = (q * eg).astype(BF16)
        k_dec_t = k_dec.T.astype(BF16)
        for h in range(GROUP):
            kdt_ref[0, rows, grp * GW + h * HEAD_DIM:grp * GW + (h + 1) * HEAD_DIM] = (
                k_dec_t[h * HEAD_DIM:(h + 1) * HEAD_DIM, :])
        gt_ref[0, c, :, sl] = jnp.exp2(g_last)


def _chunk_terms(q, k, v, bgc, expand, dmat, bdmask, chunks):
    b, t, w = q.shape
    n = t // CHUNK
    tc = chunks * CHUNK
    tok = pl.BlockSpec((1, tc, w), lambda bi, i: (bi, i, 0))
    full = lambda a: pl.BlockSpec(a.shape, lambda bi, i: (0, 0))
    per_dir_specs = [tok] * 5 + [pl.BlockSpec((1, chunks, 1, w), lambda bi, i: (bi, i, 0, 0))]
    act = jax.ShapeDtypeStruct((b, t, w), BF16)
    per_dir_shapes = [act] * 5 + [jax.ShapeDtypeStruct((b, n, 1, w), F32)]
    return pl.pallas_call(
        functools.partial(_chunk_body, chunks=chunks),
        grid=(b, t // tc),
        in_specs=[tok] * 3 + [pl.BlockSpec((1, tc, 128), lambda bi, i: (bi, i, 0)), full(expand), full(dmat),
                  full(bdmask)],
        out_specs=per_dir_specs * 2,
        out_shape=per_dir_shapes * 2,
        compiler_params=pltpu.CompilerParams(dimension_semantics=("parallel", "parallel"),
                                             vmem_limit_bytes=VMEM_LIMIT),
        name="chunk",
    )(q, k, v, bgc, expand, dmat, bdmask)


SEQ_CHUNKS = 4


def _seq_body(*refs, batch):
    fwd, bwd = refs[0:6], refs[6:12]
    s0_ref, mask_ref, of_ref, ob_ref, s_ref = refs[12:]

    @pl.when(pl.program_id(0) == 0)
    def _():
        s_ref[...] = s0_ref[...]

    bdmask = mask_ref[...]
    dirs = ((fwd, of_ref), (bwd, ob_ref))
    units = [(b, d, grp) for b in range(batch) for d in range(2) for grp in range(HEADS // GROUP)]
    lanes = lambda grp: slice(grp * GW, (grp + 1) * GW)

    for sub in range(SEQ_CHUNKS):
        chunk_of = (sub, SEQ_CHUNKS - 1 - sub)
        rows = [slice(c * CHUNK, (c + 1) * CHUNK) for c in chunk_of]
        proj = {}
        for b, d, grp in units:
            _, wk_ref, _, qd_ref, _, _ = dirs[d][0]
            lhs = jnp.concatenate([wk_ref[b, rows[d], lanes(grp)], qd_ref[b, rows[d], lanes(grp)]], axis=0)
            proj[b, d, grp] = _dot(lhs, _block_diag(s_ref[b, d, grp], bdmask))
        for b, d, grp in units:
            (u_ref, _, qk_ref, _, kdt_ref, gt_ref), o_ref = dirs[d]
            v_new = u_ref[b, rows[d], lanes(grp)].astype(F32) - proj[b, d, grp][:CHUNK]
            lhs = jnp.concatenate([qk_ref[b, rows[d], lanes(grp)], kdt_ref[b, rows[d], lanes(grp)]], axis=0)
            res = _dot(lhs, _block_diag(v_new, bdmask))
            o_ref[b, rows[d], lanes(grp)] = (proj[b, d, grp][CHUNK:] + res[:CHUNK]).astype(o_ref.dtype)
            s_ref[b, d, grp] = s_ref[b, d, grp] * gt_ref[b, chunk_of[d], :, lanes(grp)] + res[CHUNK:]


def _delta_scan(terms, s0, bdmask):
    b, t, w = terms[0].shape
    steps = t // (SEQ_CHUNKS * CHUNK)

    def specs(idx):
        return ([pl.BlockSpec((b, SEQ_CHUNKS * CHUNK, w), lambda i: (0, idx(i), 0))] * 5
                + [pl.BlockSpec((b, SEQ_CHUNKS, 1, w), lambda i: (0, idx(i), 0, 0))])

    forward, backward = (lambda i: i), (lambda i: steps - 1 - i)
    st = pl.BlockSpec(s0.shape, lambda i: (0,) * s0.ndim)
    o = jax.ShapeDtypeStruct((b, t, w), BF16)
    return pl.pallas_call(
        functools.partial(_seq_body, batch=b),
        grid=(steps,),
        in_specs=specs(forward) + specs(backward) + [st, pl.BlockSpec(bdmask.shape, lambda i: (0, 0))],
        out_specs=[specs(forward)[0], specs(backward)[0], st],
        out_shape=[o, o, jax.ShapeDtypeStruct(s0.shape, F32)],
        compiler_params=pltpu.CompilerParams(dimension_semantics=("arbitrary",), vmem_limit_bytes=VMEM_LIMIT),
        name="seq",
    )(*terms, s0, bdmask)


def _merge_body(x_ref, ona_ref, gzna_ref, of_ref, ob_ref, gzdn_ref, sgna_ref, sgdn_ref, gate_ref, dnw_ref, e_ref,
                wna_ref, wdn_ref, wout_ref, o_ref):
    a = (ona_ref[0].astype(F32) * gzna_ref[0].astype(F32)).astype(BF16)
    u_na = _dot(a, wna_ref[...])
    od = of_ref[0].astype(F32) + ob_ref[0].astype(F32)
    odn = od * lax.rsqrt(_seg_sum(od * od, e_ref) * (1.0 / HEAD_DIM) + EPS) * dnw_ref[...]
    u_dn = _dot((odn * gzdn_ref[0].astype(F32)).astype(BF16), wdn_ref[...])
    y = sgna_ref[0].astype(F32) * u_na + sgdn_ref[0].astype(F32) * u_dn
    o_ref[0] = x_ref[0] + gate_ref[0] * _dot(y.astype(BF16), wout_ref[...])


def _merge(x, o_na, gz_na, o_f, o_b, gz_dn, sg_na, sg_dn, gate, dnw, e64, w_o_na, w_o_dn, w_out, tm):
    b, t, d = x.shape
    w = WIDTH
    tok = lambda width: pl.BlockSpec((1, tm, width), lambda bi, i: (bi, i, 0))
    full = lambda a: pl.BlockSpec(a.shape, lambda bi, i: (0, 0))
    return pl.pallas_call(
        _merge_body,
        grid=(b, t // tm),
        in_specs=[tok(d), tok(w), tok(w), tok(w), tok(w), tok(w), tok(d), tok(d),
                  pl.BlockSpec((1, 1, d), lambda bi, i: (bi, 0, 0)), full(dnw), full(e64),
                  full(w_o_na), full(w_o_dn), full(w_out)],
        out_specs=tok(d),
        out_shape=jax.ShapeDtypeStruct((b, t, d), F32),
        compiler_params=pltpu.CompilerParams(dimension_semantics=("parallel", "parallel"),
                                             vmem_limit_bytes=VMEM_LIMIT),
        name="merge",
    )(x, o_na, gz_na, o_f, o_b, gz_dn, sg_na, sg_dn, gate, dnw, e64, w_o_na, w_o_dn, w_out)


def _constants(tc):
    seg = np.arange(GW) // HEAD_DIM
    e64 = (seg[:, None] == seg[None, :]).astype(np.float32)
    tok = np.arange(tc)
    same_chunk = (tok[:, None] // CHUNK) == (tok[None, :] // CHUNK)
    tril = (same_chunk & (tok[None, :] <= tok[:, None])).astype(np.float32)
    triu = (same_chunk & (tok[None, :] >= tok[:, None])).astype(np.float32)
    expand = np.zeros((128, 4 * WIDTH), np.float32)
    for s in range(4):
        for h in range(HEADS):
            expand[s * HEADS + h, s * WIDTH + h * HEAD_DIM:s * WIDTH + (h + 1) * HEAD_DIM] = 1.0
    lane = np.arange(GW)
    dmat = (np.arange(CHUNK)[:, None] - (lane % HEAD_DIM)[None, :]).astype(np.int32)
    bdmask = ((lane[:, None] // HEAD_DIM) == (lane[None, :] // HEAD_DIM)).astype(np.float32)
    return (jnp.asarray(e64, BF16), jnp.asarray(tril, BF16), jnp.asarray(triu, BF16), jnp.asarray(expand, BF16),
            jnp.asarray(dmat), jnp.asarray(bdmask, BF16))


def _rope_tables(t):
    half = HEAD_DIM // 4
    rows = t // GRID_W
    d = jnp.arange(2 * HEAD_DIM) % HEAD_DIM
    freqs = ROPE_BASE ** (-(d % half).astype(F32) / half)
    by_row = (d < HEAD_DIM // 2)[None, None, :]
    ang_r = jnp.arange(rows).astype(F32)[:, None] * freqs[None, :]
    ang_c = jnp.arange(GRID_W).astype(F32)[:, None] * freqs[None, :]
    grid = lambda f: jnp.where(by_row, f(ang_r)[:, None, :], f(ang_c)[None, :, :]).reshape(t, 2 * HEAD_DIM)
    cos, sin = grid(jnp.cos), grid(jnp.sin)
    lower = ((d % (2 * half)) < half)[None, :]
    return cos, jnp.where(lower, -sin, 0.0), jnp.where(lower, 0.0, sin)


def _pad_lanes(a, offset):
    return jnp.zeros((1, 128), F32).at[0, offset:offset + a.size].set(a.reshape(-1))


def _layer(x, ctx, c, c_ctx, mod_w, mod_b, norm_w, w_in, conv_w, na_q_norm, na_k_norm, na_rpb, dn_A_log, dn_dt_bias,
           dn_norm_w, w_o_na, w_o_dn, w_out):
    b, t, d = x.shape
    w = WIDTH
    tm = 256
    e64, tril, triu, expand, dmat, bdmask = _constants(CUM_ROWS)

    cc = jnp.zeros((8, d), F32).at[:b].set(c).at[b].set(c_ctx)
    mod = _modulation(cc, mod_w, mod_b)
    shift, scale, gate = mod[:, :d], mod[:, d:2 * d], mod[:, 2 * d:]
    rows_x = lambda a: a[:b, None, :]
    rows_c = lambda a: jnp.broadcast_to(a[b][None, None, :], (b, 1, d))

    w_all = w_in.astype(BF16)
    w_tail = w_all[:, 8 * w + 4 * HEADS:]
    nw = norm_w.reshape(1, d)
    qw = jnp.tile(na_q_norm, HEADS).reshape(1, w)
    kw = jnp.tile(na_k_norm, HEADS).reshape(1, w)
    alog = _pad_lanes(dn_A_log, 2 * HEADS)
    dtb = _pad_lanes(dn_dt_bias, 2 * HEADS)
    project = functools.partial(_project, nw=nw, w_all=w_all, w_tail=w_tail, qw=qw, kw=kw, e64=e64, alog=alog,
                                dtb=dtb, conv_w=conv_w, tril=tril, triu=triu)
    q_na, gz_na, gz_dn, sg_na, sg_dn, k_na, v_na, *dn = project(x, rows_x(scale), rows_x(shift),
                                                                 tables=_rope_tables(t), latent=True, tm=2 * tm)
    k_c, v_c, *dn_c = project(ctx, rows_c(scale), rows_c(shift), tables=None, latent=False, tm=tm)

    bias = _bias_table(na_rpb, t // GRID_W)
    o_na = _neighbourhood_attention(q_na, k_na, v_na, k_c, v_c, bias)

    s_zero = jnp.zeros((b, 2, HEADS // GROUP, HEAD_DIM, GW), F32)
    _, _, s_ctx = _delta_scan(_chunk_terms(*dn_c, expand, dmat, bdmask, chunks=4), s_zero, bdmask)
    o_f, o_b, _ = _delta_scan(_chunk_terms(*dn, expand, dmat, bdmask, chunks=8), s_ctx, bdmask)

    dnw = jnp.tile(dn_norm_w, HEADS).reshape(1, w)
    return _merge(x, o_na, gz_na, o_f, o_b, gz_dn, sg_na, sg_dn, rows_x(gate), dnw, e64,
                  w_o_na.astype(BF16), w_o_dn.astype(BF16), w_out.astype(BF16), 4 * tm)


def kernel(x, c, ctx, c_ctx, mod_w, mod_b, norm_w, w_in, conv_w, na_q_norm, na_k_norm, na_rpb, dn_A_log, dn_dt_bias,
           dn_norm_w, w_o_na, w_o_dn, w_out):
    depth = mod_w.shape[0]
    assert depth == 1, "context-stream update between layers is not implemented"
    return _layer(x, ctx, c, c_ctx, mod_w[0], mod_b[0], norm_w[0], w_in[0], conv_w[0], na_q_norm[0], na_k_norm[0],
                  na_rpb[0], dn_A_log[0], dn_dt_bias[0], dn_norm_w[0], w_o_na[0], w_o_dn[0], w_out[0])
```
